```python
import jax, jax.numpy as jnp
from jax import lax
import numpy as np

D_MODEL = 1024
BATCH = 4
SEQ = 4096
DEPTH = 4

GRID_W = 64
CTX_LEN = 256

MLA_HEADS = 8
Q_LORA = 384
KV_LORA = 256
QK_NOPE = 64
QK_ROPE = 32
V_DIM = 64
ROPE_AXIS_FREQ = QK_ROPE // 4
ROPE_BASE = 10000.0
Q_BLOCK = 128

CONV_WIDTH = 512
CONV_K = 31

FNET_WIDTH = 512
FNET_GROUPS = 4

LRU_WIDTH = 512
LRU_HEADS = 8
LRU_CONV_K = 4
LRU_C = 8.0

N_BRANCH = 4
BRANCH_WIDTH = 512

N_EXPERTS = 16
EXPERT_FF = 1408
EC_CAPACITY = 2

DEEPNORM_ALPHA = (2 * DEPTH) ** 0.25
DEEPNORM_BETA = (8 * DEPTH) ** -0.25
LN_EPS = 1e-6

CTX_STATE_SPLITS = (KV_LORA, QK_ROPE, LRU_WIDTH)
IN_SPLITS = (KV_LORA, QK_ROPE, LRU_WIDTH, Q_LORA, 2 * CONV_WIDTH, FNET_WIDTH, LRU_WIDTH, N_BRANCH * D_MODEL)
CTX_STATE_WIDTH = KV_LORA + QK_ROPE + LRU_WIDTH
IN_WIDTH = CTX_STATE_WIDTH + Q_LORA + 2 * CONV_WIDTH + FNET_WIDTH + LRU_WIDTH + N_BRANCH * D_MODEL

kernel_name = 'hybrid_diffusion_mla_conformer_fnet_rglru_ecmoe'


def split_cols(p, sizes):
    idx, acc = [], 0
    for s in sizes[:-1]:
        acc += s
        idx.append(acc)
    return jnp.split(p, idx, axis=-1)


def layer_norm(x, g=None, b=None):
    xf = x.astype(jnp.float32)
    mu = jnp.mean(xf, axis=-1, keepdims=True)
    var = jnp.mean(jnp.square(xf - mu), axis=-1, keepdims=True)
    y = (xf - mu) * lax.rsqrt(var + LN_EPS)
    if g is not None:
        y = y * g.astype(jnp.float32) + b.astype(jnp.float32)
    return y.astype(x.dtype)


def rms_norm(x, g):
    xf = x.astype(jnp.float32)
    y = xf * lax.rsqrt(jnp.mean(jnp.square(xf), axis=-1, keepdims=True) + LN_EPS) * g.astype(jnp.float32)
    return y.astype(x.dtype)


def modulate(x, shift, scale):
    return layer_norm(x) * (1.0 + scale) + shift


def post_norm(x, out, gate, g, b):
    return layer_norm(DEEPNORM_ALPHA * x + gate * out, g, b)


def rope_tables(rows):
    row = jnp.repeat(jnp.arange(rows), GRID_W)
    col = jnp.tile(jnp.arange(GRID_W), rows)
    inv = ROPE_BASE ** (-jnp.arange(ROPE_AXIS_FREQ, dtype=jnp.float32) / ROPE_AXIS_FREQ)
    ang_r = row.astype(jnp.float32)[:, None, None] * inv
    ang_c = col.astype(jnp.float32)[:, None, None] * inv
    return (jnp.cos(ang_r), jnp.sin(ang_r), jnp.cos(ang_c), jnp.sin(ang_c))


def _rotate(x, cos, sin):
    half = x.shape[-1] // 2
    x1, x2 = x[..., :half], x[..., half:]
    return jnp.concatenate([x1 * cos - x2 * sin, x1 * sin + x2 * cos], axis=-1)


def apply_rope2d(x, tabs):
    cos_r, sin_r, cos_c, sin_c = tabs
    xf = x.astype(jnp.float32)
    r = QK_ROPE // 2
    y = jnp.concatenate([_rotate(xf[..., :r], cos_r, sin_r), _rotate(xf[..., r:], cos_c, sin_c)], axis=-1)
    return y.astype(x.dtype)


def mla_queries(cq_raw, q_norm_g, w_uq, tabs):
    q = jnp.einsum('bsr,rhd->bshd', rms_norm(cq_raw, q_norm_g), w_uq)
    q_nope, q_rope = q[..., :QK_NOPE], q[..., QK_NOPE:]
    if tabs is not None:
        q_rope = apply_rope2d(q_rope, tabs)
    return jnp.concatenate([q_nope, q_rope], axis=-1)


def mla_keys_values(ckv_raw, kr_raw, kv_norm_g, w_ukv, tabs):
    kv = jnp.einsum('bsr,rhd->bshd', rms_norm(ckv_raw, kv_norm_g), w_ukv)
    k_nope, v = kv[..., :QK_NOPE], kv[..., QK_NOPE:]
    k_rope = kr_raw[:, :, None, :]
    if tabs is not None:
        k_rope = apply_rope2d(k_rope, tabs)
    k_rope = jnp.broadcast_to(k_rope, k_nope.shape[:-1] + (QK_ROPE,))
    return jnp.concatenate([k_nope, k_rope], axis=-1), v


def attend(q, k, v):
    scale = (QK_NOPE + QK_ROPE) ** -0.5
    s = jnp.einsum('bqhd,bkhd->bhqk', q, k).astype(jnp.float32) * scale
    p = jax.nn.softmax(s, axis=-1).astype(v.dtype)
    o = jnp.einsum('bhqk,bkhd->bqhd', p, v)
    return o.reshape(o.shape[0], o.shape[1], -1)


def attend_blocked(q, k, v):
    B, S, H, d = q.shape
    nb = S // Q_BLOCK
    qb = jnp.moveaxis(q.reshape(B, nb, Q_BLOCK, H, d), 1, 0)
    o = lax.map(lambda qi: attend(qi, k, v), qb)
    return jnp.moveaxis(o, 0, 1).reshape(B, S, H * V_DIM)


def depthwise_conv(x, w, b, pad):
    y = lax.conv_general_dilated(x, w[:, None, :], window_strides=(1,), padding=[pad],
                                 dimension_numbers=('NWC', 'WIO', 'NWC'),
                                 feature_group_count=x.shape[-1])
    return y + b


def conformer_branch(glu_in, lp):
    a, g = jnp.split(glu_in, 2, axis=-1)
    u = a * jax.nn.sigmoid(g)
    u = depthwise_conv(u, lp['cv_w'], lp['cv_b'], ((CONV_K - 1) // 2, (CONV_K - 1) // 2))
    return jax.nn.silu(layer_norm(u, lp['cv_ln_g'], lp['cv_ln_b']))


def fnet_branch(f):
    B, S, _ = f.shape
    fg = f.astype(jnp.float32).reshape(B, S, FNET_GROUPS, FNET_WIDTH // FNET_GROUPS)
    y = jnp.fft.fft2(fg, axes=(1, 3), norm='ortho').real
    return y.reshape(B, S, FNET_WIDTH).astype(f.dtype)


def _lin_combine(left, right):
    a1, b1 = left
    a2, b2 = right
    return a1 * a2, a2 * b1 + b2


def rglru_scan(xc, wa, ba, wx, bx, lam, h0):
    B, S, _ = xc.shape
    xh = xc.reshape(B, S, LRU_HEADS, LRU_WIDTH // LRU_HEADS)
    r = jax.nn.sigmoid((jnp.einsum('bshi,hij->bshj', xh, wa).reshape(B, S, LRU_WIDTH) + ba).astype(jnp.float32))
    i = jax.nn.sigmoid((jnp.einsum('bshi,hij->bshj', xh, wx).reshape(B, S, LRU_WIDTH) + bx).astype(jnp.float32))
    log_a = -LRU_C * jax.nn.softplus(-lam.astype(jnp.float32)) * r
    a = jnp.exp(log_a)
    u = jnp.sqrt(-jnp.expm1(2.0 * log_a)) * (i * xc.astype(jnp.float32))
    u = u.at[:, 0].add(a[:, 0] * h0)
    _, h = lax.associative_scan(_lin_combine, (a, u), axis=1)
    return h, h[:, -1]


def rglru_bidir(xb, lp, h0_fwd, h0_bwd):
    xc = depthwise_conv(xb, lp['lru_conv_w'], lp['lru_conv_b'], (LRU_CONV_K // 2, LRU_CONV_K - 1 - LRU_CONV_K // 2))
    hf, s_fwd = rglru_scan(xc, lp['lru_wa'][0], lp['lru_ba'][0], lp['lru_wx'][0], lp['lru_bx'][0],
                           lp['lru_lambda'][0], h0_fwd)
    hb, s_bwd = rglru_scan(jnp.flip(xc, 1), lp['lru_wa'][1], lp['lru_ba'][1], lp['lru_wx'][1], lp['lru_bx'][1],
                           lp['lru_lambda'][1], h0_bwd)
    h = (hf + jnp.flip(hb, 1)).astype(xb.dtype)
    return h, s_fwd, s_bwd


def merge_branches(branches, gate_logits, w_branch, w_out):
    stacked = jnp.stack(branches, axis=2)
    proj = jnp.einsum('bskc,kcd->bskd', stacked, w_branch)
    gates = jax.nn.sigmoid(gate_logits.reshape(proj.shape))
    return jnp.sum(gates * proj, axis=2) @ w_out


def mixer_sublayer(x_lat, x_ctx, mod_l, mod_c, lp, tabs, ctx_out):
    shift_l, scale_l, gate_l = mod_l
    B = x_lat.shape[0]
    h_c = modulate(x_ctx, mod_c[0], mod_c[1])
    if ctx_out:
        p_c = split_cols(h_c @ lp['w_in'], IN_SPLITS)
    else:
        p_c = split_cols(h_c @ lp['w_in'][:, :CTX_STATE_WIDTH], CTX_STATE_SPLITS)
    k_c, v_c = mla_keys_values(p_c[0], p_c[1], lp['kv_norm_g'], lp['w_ukv'], None)
    zero_state = jnp.zeros((B, LRU_WIDTH), jnp.float32)
    hrec_c, s_fwd, s_bwd = rglru_bidir(p_c[2], lp, zero_state, zero_state)

    h_l = modulate(x_lat, shift_l, scale_l)
    ckv_l, kr_l, xb_l, cq_l, glu_l, f_l, gb_l, gl_l = split_cols(h_l @ lp['w_in'], IN_SPLITS)
    q_l = mla_queries(cq_l, lp['q_norm_g'], lp['w_uq'], tabs)
    k_l, v_l = mla_keys_values(ckv_l, kr_l, lp['kv_norm_g'], lp['w_ukv'], tabs)
    attn_l = attend_blocked(q_l, jnp.concatenate([k_c, k_l], axis=1), jnp.concatenate([v_c, v_l], axis=1))
    hrec_l, _, _ = rglru_bidir(xb_l, lp, s_fwd, s_bwd)
    branches_l = [attn_l, conformer_branch(glu_l, lp), fnet_branch(f_l), hrec_l * jax.nn.gelu(gb_l)]
    out_l = merge_branches(branches_l, gl_l, lp['w_branch'], lp['w_out'])
    new_lat = post_norm(x_lat, out_l, gate_l, lp['ln1_g'], lp['ln1_b'])
    if not ctx_out:
        return new_lat, None

    cq_c, glu_c, f_c, gb_c, gl_c = p_c[3], p_c[4], p_c[5], p_c[6], p_c[7]
    attn_c = attend(mla_queries(cq_c, lp['q_norm_g'], lp['w_uq'], None), k_c, v_c)
    branches_c = [attn_c, conformer_branch(glu_c, lp), fnet_branch(f_c), hrec_c * jax.nn.gelu(gb_c)]
    out_c = merge_branches(branches_c, gl_c, lp['w_branch'], lp['w_out'])
    new_ctx = post_norm(x_ctx, out_c, mod_c[2], lp['ln1_g'], lp['ln1_b'])
    return new_lat, new_ctx


def ec_moe(h, w_router, w_e_gate, w_e_up, w_e_down):
    B, N, _ = h.shape
    cap = EC_CAPACITY * N // N_EXPERTS
    affinity = jax.nn.softmax((h @ w_router).astype(jnp.float32), axis=-1)
    top_w, top_i = lax.top_k(jnp.swapaxes(affinity, 1, 2), cap)
    b_idx = jnp.arange(B)[:, None, None]
    xe = h[b_idx, top_i]
    he = jax.nn.silu(jnp.einsum('becd,edf->becf', xe, w_e_gate)) * jnp.einsum('becd,edf->becf', xe, w_e_up)
    ye = jnp.einsum('becf,efd->becd', he, w_e_down) * top_w[..., None].astype(h.dtype)
    return jnp.zeros_like(h).at[b_idx, top_i].add(ye)


def moe_sublayer(x, shift, scale, gate, lp):
    h = modulate(x, shift, scale)
    out = ec_moe(h, lp['w_router'], lp['w_e_gate'], lp['w_e_up'], lp['w_e_down'])
    return post_norm(x, out, gate, lp['ln2_g'], lp['ln2_b'])


def setup_inputs(seed: int = 0) -> dict:
    key = jax.random.key(seed)
    ks = jax.random.split(key, 32)
    f32 = jnp.float32
    L = DEPTH
    hd = LRU_WIDTH // LRU_HEADS

    def nrm(k, shape, scale):
        return jax.random.normal(k, shape, f32) * scale

    def gain(k, shape):
        return 1.0 + 0.01 * jax.random.normal(k, shape, f32)

    def bias(k, shape):
        return 0.01 * jax.random.normal(k, shape, f32)

    u = jax.random.uniform(ks[20], (L, 2, LRU_WIDTH), f32, 0.9, 0.999)
    a_base = u ** (1.0 / LRU_C)
    lam = jnp.log(a_base) - jnp.log1p(-a_base)
    return {
        'x': nrm(ks[0], (BATCH, SEQ, D_MODEL), 1.0),
        'c': nrm(ks[1], (BATCH, D_MODEL), 1.0),
        'ctx': nrm(ks[2], (BATCH, CTX_LEN, D_MODEL), 1.0),
        'c_ctx': nrm(ks[3], (D_MODEL,), 1.0),
        'ada_w': nrm(ks[4], (L, D_MODEL, 6 * D_MODEL), 0.5 * D_MODEL ** -0.5),
        'ada_b': bias(ks[5], (L, 6 * D_MODEL)),
        'w_in': nrm(ks[6], (L, D_MODEL, IN_WIDTH), D_MODEL ** -0.5),
        'q_norm_g': gain(ks[7], (L, Q_LORA)),
        'w_uq': nrm(ks[8], (L, Q_LORA, MLA_HEADS, QK_NOPE + QK_ROPE), Q_LORA ** -0.5),
        'kv_norm_g': gain(ks[9], (L, KV_LORA)),
        'w_ukv': nrm(ks[10], (L, KV_LORA, MLA_HEADS, QK_NOPE + V_DIM), KV_LORA ** -0.5),
        'cv_w': nrm(ks[11], (L, CONV_K, CONV_WIDTH), CONV_K ** -0.5),
        'cv_b': bias(ks[12], (L, CONV_WIDTH)),
        'cv_ln_g': gain(ks[13], (L, CONV_WIDTH)),
        'cv_ln_b': bias(ks[14], (L, CONV_WIDTH)),
        'lru_conv_w': nrm(ks[15], (L, LRU_CONV_K, LRU_WIDTH), LRU_CONV_K ** -0.5),
        'lru_conv_b': bias(ks[16], (L, LRU_WIDTH)),
        'lru_wa': nrm(ks[17], (L, 2, LRU_HEADS, hd, hd), hd ** -0.5),
        'lru_ba': bias(ks[18], (L, 2, LRU_WIDTH)),
        'lru_wx': nrm(ks[19], (L, 2, LRU_HEADS, hd, hd), hd ** -0.5),
        'lru_bx': bias(ks[21], (L, 2, LRU_WIDTH)),
        'lru_lambda': lam,
        'w_branch': nrm(ks[22], (L, N_BRANCH, BRANCH_WIDTH, D_MODEL), BRANCH_WIDTH ** -0.5),
        'w_out': nrm(ks[23], (L, D_MODEL, D_MODEL), DEEPNORM_BETA * D_MODEL ** -0.5),
        'ln1_g': gain(ks[24], (L, D_MODEL)),
        'ln1_b': bias(ks[25], (L, D_MODEL)),
        'w_router': nrm(ks[26], (L, D_MODEL, N_EXPERTS), D_MODEL ** -0.5),
        'w_e_gate': nrm(ks[27], (L, N_EXPERTS, D_MODEL, EXPERT_FF), D_MODEL ** -0.5),
        'w_e_up': nrm(ks[28], (L, N_EXPERTS, D_MODEL, EXPERT_FF), D_MODEL ** -0.5),
        'w_e_down': nrm(ks[29], (L, N_EXPERTS, EXPERT_FF, D_MODEL), DEEPNORM_BETA * EXPERT_FF ** -0.5),
        'ln2_g': gain(ks[30], (L, D_MODEL)),
        'ln2_b': bias(ks[31], (L, D_MODEL)),
    }


def reference(x, c, ctx, c_ctx, ada_w, ada_b, w_in, q_norm_g, w_uq, kv_norm_g, w_ukv, cv_w, cv_b, cv_ln_g,
              cv_ln_b, lru_conv_w, lru_conv_b, lru_wa, lru_ba, lru_wx, lru_bx, lru_lambda, w_branch, w_out,
              ln1_g, ln1_b, w_router, w_e_gate, w_e_up, w_e_down, ln2_g, ln2_b):
    ROWS = x.shape[1] // GRID_W
    tabs = rope_tables(ROWS)
    x_lat, x_ctx = x, ctx
    for l in range(DEPTH):
        ctx_out = l < DEPTH - 1
        lp = {
            'w_in': w_in[l], 'q_norm_g': q_norm_g[l], 'w_uq': w_uq[l], 'kv_norm_g': kv_norm_g[l],
            'w_ukv': w_ukv[l], 'cv_w': cv_w[l], 'cv_b': cv_b[l], 'cv_ln_g': cv_ln_g[l], 'cv_ln_b': cv_ln_b[l],
            'lru_conv_w': lru_conv_w[l], 'lru_conv_b': lru_conv_b[l], 'lru_wa': lru_wa[l], 'lru_ba': lru_ba[l],
            'lru_wx': lru_wx[l], 'lru_bx': lru_bx[l], 'lru_lambda': lru_lambda[l], 'w_branch': w_branch[l],
            'w_out': w_out[l], 'ln1_g': ln1_g[l], 'ln1_b': ln1_b[l], 'w_router': w_router[l],
            'w_e_gate': w_e_gate[l], 'w_e_up': w_e_up[l], 'w_e_down': w_e_down[l],
            'ln2_g': ln2_g[l], 'ln2_b': ln2_b[l],
        }
        mod = (jax.nn.silu(c) @ ada_w[l] + ada_b[l])[:, None, :]
        shift1, scale1, gate1, shift2, scale2, gate2 = jnp.split(mod, 6, axis=-1)
        n_mod_c = 6 if ctx_out else 2
        mod_c = jax.nn.silu(c_ctx) @ ada_w[l][:, :n_mod_c * D_MODEL] + ada_b[l][:n_mod_c * D_MODEL]
        mods_c = jnp.split(mod_c, n_mod_c)
        x_lat, x_ctx_new = mixer_sublayer(x_lat, x_ctx, (shift1, scale1, gate1), mods_c, lp, tabs, ctx_out)
        x_lat = moe_sublayer(x_lat, shift2, scale2, gate2, lp)
        if ctx_out:
            x_ctx = moe_sublayer(x_ctx_new, mods_c[3], mods_c[4], mods_c[5], lp)
    return x_lat
```

```python
import functools

import jax
import jax.numpy as jnp
from jax import lax
from jax.experimental import pallas as pl
from jax.experimental.pallas import tpu as pltpu

D_MODEL = 1024
DEPTH = 4
GRID_W = 64
MLA_HEADS = 8
Q_LORA = 384
KV_LORA = 256
QK_NOPE = 64
QK_ROPE = 32
V_DIM = 64
ROPE_AXIS_FREQ = QK_ROPE // 4
ROPE_BASE = 10000.0
Q_BLOCK = 128
CONV_WIDTH = 512
CONV_K = 31
FNET_WIDTH = 512
FNET_GROUPS = 4
LRU_WIDTH = 512
LRU_HEADS = 8
LRU_CONV_K = 4
LRU_C = 8.0
N_BRANCH = 4
N_EXPERTS = 16
EC_CAPACITY = 2
DEEPNORM_ALPHA = (2 * DEPTH) ** 0.25
LN_EPS = 1e-6
CTX_STATE_SPLITS = (KV_LORA, QK_ROPE, LRU_WIDTH)
IN_SPLITS = (KV_LORA, QK_ROPE, LRU_WIDTH, Q_LORA, 2 * CONV_WIDTH, FNET_WIDTH, LRU_WIDTH, N_BRANCH * D_MODEL)
CTX_STATE_WIDTH = KV_LORA + QK_ROPE + LRU_WIDTH

V7X_LANES = 128
V7X_VMEM_LIMIT_BYTES = 56 * 1024 * 1024


def _matmul_body(a_ref, b_ref, o_ref):
    o_ref[...] = jnp.dot(a_ref[...].astype(jnp.bfloat16), b_ref[...],
                         preferred_element_type=jnp.float32)


def _pick_tile(n, cap, quantum):
    best = quantum
    t = quantum
    while t <= min(n, cap):
        if n % t == 0:
            best = t
        t += quantum
    return best


def pmatmul(a, b):
    M, K = a.shape
    N = b.shape[1]
    n_pad = (-N) % V7X_LANES
    if n_pad:
        b = jnp.pad(b, ((0, 0), (0, n_pad)))
    Np = N + n_pad
    tm = _pick_tile(M, 512, 8)
    tn = _pick_tile(Np, 2048, V7X_LANES)
    out = pl.pallas_call(
        _matmul_body,
        grid=(Np // tn, M // tm),
        in_specs=[pl.BlockSpec((tm, K), lambda j, i: (i, 0)),
                  pl.BlockSpec((K, tn), lambda j, i: (0, j))],
        out_specs=pl.BlockSpec((tm, tn), lambda j, i: (i, j)),
        out_shape=jax.ShapeDtypeStruct((M, Np), jnp.float32),
        compiler_params=pltpu.CompilerParams(
            dimension_semantics=("arbitrary", "arbitrary"),
            vmem_limit_bytes=V7X_VMEM_LIMIT_BYTES),
        name="pmatmul",
    )(a, b)
    return out[:, :N] if n_pad else out


def _mm3(x, w):
    B, S, K = x.shape
    return pmatmul(x.reshape(B * S, K), w.astype(jnp.bfloat16)).reshape(B, S, -1)


def split_cols(p, sizes):
    idx, acc = [], 0
    for s in sizes[:-1]:
        acc += s
        idx.append(acc)
    return jnp.split(p, idx, axis=-1)


def layer_norm(x, g=None, b=None):
    mu = jnp.mean(x, axis=-1, keepdims=True)
    var = jnp.mean(jnp.square(x - mu), axis=-1, keepdims=True)
    y = (x - mu) * lax.rsqrt(var + LN_EPS)
    if g is not None:
        y = y * g + b
    return y


def rms_norm(x, g):
    return x * lax.rsqrt(jnp.mean(jnp.square(x), axis=-1, keepdims=True) + LN_EPS) * g


def modulate(x, shift, scale):
    return layer_norm(x) * (1.0 + scale) + shift


def post_norm(x, out, gate, g, b):
    return layer_norm(DEEPNORM_ALPHA * x + gate * out, g, b)


def rope_tables(rows):
    row = jnp.repeat(jnp.arange(rows), GRID_W)
    col = jnp.tile(jnp.arange(GRID_W), rows)
    inv = ROPE_BASE ** (-jnp.arange(ROPE_AXIS_FREQ, dtype=jnp.float32) / ROPE_AXIS_FREQ)
    ang_r = row.astype(jnp.float32)[:, None, None] * inv
    ang_c = col.astype(jnp.float32)[:, None, None] * inv
    return (jnp.cos(ang_r), jnp.sin(ang_r), jnp.cos(ang_c), jnp.sin(ang_c))


def _rotate(x, cos, sin):
    half = x.shape[-1] // 2
    x1, x2 = x[..., :half], x[..., half:]
    return jnp.concatenate([x1 * cos - x2 * sin, x1 * sin + x2 * cos], axis=-1)


def apply_rope2d(x, tabs):
    cos_r, sin_r, cos_c, sin_c = tabs
    r = QK_ROPE // 2
    return jnp.concatenate([_rotate(x[..., :r], cos_r, sin_r), _rotate(x[..., r:], cos_c, sin_c)], axis=-1)


def mla_queries(cq_raw, q_norm_g, w_uq, tabs):
    q = jnp.einsum('bsr,rhd->bshd', rms_norm(cq_raw, q_norm_g), w_uq)
    q_nope, q_rope = q[..., :QK_NOPE], q[..., QK_NOPE:]
    if tabs is not None:
        q_rope = apply_rope2d(q_rope, tabs)
    return jnp.concatenate([q_nope, q_rope], axis=-1)


def mla_keys_values(ckv_raw, kr_raw, kv_norm_g, w_ukv, tabs):
    kv = jnp.einsum('bsr,rhd->bshd', rms_norm(ckv_raw, kv_norm_g), w_ukv)
    k_nope, v = kv[..., :QK_NOPE], kv[..., QK_NOPE:]
    k_rope = kr_raw[:, :, None, :]
    if tabs is not None:
        k_rope = apply_rope2d(k_rope, tabs)
    k_rope = jnp.broadcast_to(k_rope, k_nope.shape[:-1] + (QK_ROPE,))
    return jnp.concatenate([k_nope, k_rope], axis=-1), v


def attend(q, k, v):
    scale = (QK_NOPE + QK_ROPE) ** -0.5
    s = jnp.einsum('bqhd,bkhd->bhqk', q, k).astype(jnp.float32) * scale
    p = jax.nn.softmax(s, axis=-1).astype(v.dtype)
    o = jnp.einsum('bhqk,bkhd->bqhd', p, v)
    return o.reshape(o.shape[0], o.shape[1], -1)


def attend_blocked(q, k, v):
    B, S, H, d = q.shape
    nb = S // Q_BLOCK
    qb = jnp.moveaxis(q.reshape(B, nb, Q_BLOCK, H, d), 1, 0)
    o = lax.map(lambda qi: attend(qi, k, v), qb)
    return jnp.moveaxis(o, 0, 1).reshape(B, S, H * V_DIM)


def depthwise_conv(x, w, b, pad):
    y = lax.conv_general_dilated(x, w[:, None, :], window_strides=(1,), padding=[pad],
                                 dimension_numbers=('NWC', 'WIO', 'NWC'),
                                 feature_group_count=x.shape[-1])
    return y + b


def conformer_branch(glu_in, lp):
    a, g = jnp.split(glu_in, 2, axis=-1)
    u = a * jax.nn.sigmoid(g)
    u = depthwise_conv(u, lp['cv_w'], lp['cv_b'], ((CONV_K - 1) // 2, (CONV_K - 1) // 2))
    return jax.nn.silu(layer_norm(u, lp['cv_ln_g'], lp['cv_ln_b']))


def fnet_branch(f):
    B, S, _ = f.shape
    fg = f.reshape(B, S, FNET_GROUPS, FNET_WIDTH // FNET_GROUPS)
    y = jnp.fft.fft2(fg, axes=(1, 3), norm='ortho').real
    return y.reshape(B, S, FNET_WIDTH)


def _lin_combine(left, right):
    a1, b1 = left
    a2, b2 = right
    return a1 * a2, a2 * b1 + b2


def rglru_scan(xc, wa, ba, wx, bx, lam, h0):
    B, S, _ = xc.shape
    xh = xc.reshape(B, S, LRU_HEADS, LRU_WIDTH // LRU_HEADS)
    r = jax.nn.sigmoid(jnp.einsum('bshi,hij->bshj', xh, wa).reshape(B, S, LRU_WIDTH) + ba)
    i = jax.nn.sigmoid(jnp.einsum('bshi,hij->bshj', xh, wx).reshape(B, S, LRU_WIDTH) + bx)
    log_a = -LRU_C * jax.nn.softplus(-lam) * r
    a = jnp.exp(log_a)
    u = jnp.sqrt(-jnp.expm1(2.0 * log_a)) * (i * xc)
    u = u.at[:, 0].add(a[:, 0] * h0)
    _, h = lax.associative_scan(_lin_combine, (a, u), axis=1)
    return h, h[:, -1]


def rglru_bidir(xb, lp, h0_fwd, h0_bwd):
    xc = depthwise_conv(xb, lp['lru_conv_w'], lp['lru_conv_b'], (LRU_CONV_K // 2, LRU_CONV_K - 1 - LRU_CONV_K // 2))
    hf, s_fwd = rglru_scan(xc, lp['lru_wa'][0], lp['lru_ba'][0], lp['lru_wx'][0], lp['lru_bx'][0],
                           lp['lru_lambda'][0], h0_fwd)
    hb, s_bwd = rglru_scan(jnp.flip(xc, 1), lp['lru_wa'][1], lp['lru_ba'][1], lp['lru_wx'][1], lp['lru_bx'][1],
                           lp['lru_lambda'][1], h0_bwd)
    return hf + jnp.flip(hb, 1), s_fwd, s_bwd


def merge_branches(branches, gate_logits, w_branch, w_out):
    stacked = jnp.stack(branches, axis=2)
    proj = jnp.einsum('bskc,kcd->bskd', stacked, w_branch)
    gates = jax.nn.sigmoid(gate_logits.reshape(proj.shape))
    return _mm3(jnp.sum(gates * proj, axis=2), w_out)


def mixer_sublayer(x_lat, x_ctx, mod_l, mod_c, lp, tabs, ctx_out):
    shift_l, scale_l, gate_l = mod_l
    B = x_lat.shape[0]
    h_c = modulate(x_ctx, mod_c[0], mod_c[1])
    if ctx_out:
        p_c = split_cols(_mm3(h_c, lp['w_in']), IN_SPLITS)
    else:
        p_c = split_cols(_mm3(h_c, lp['w_in'][:, :CTX_STATE_WIDTH]), CTX_STATE_SPLITS)
    k_c, v_c = mla_keys_values(p_c[0], p_c[1], lp['kv_norm_g'], lp['w_ukv'], None)
    zero_state = jnp.zeros((B, LRU_WIDTH), jnp.float32)
    hrec_c, s_fwd, s_bwd = rglru_bidir(p_c[2], lp, zero_state, zero_state)

    h_l = modulate(x_lat, shift_l, scale_l)
    ckv_l, kr_l, xb_l, cq_l, glu_l, f_l, gb_l, gl_l = split_cols(_mm3(h_l, lp['w_in']), IN_SPLITS)
    q_l = mla_queries(cq_l, lp['q_norm_g'], lp['w_uq'], tabs)
    k_l, v_l = mla_keys_values(ckv_l, kr_l, lp['kv_norm_g'], lp['w_ukv'], tabs)
    attn_l = attend_blocked(q_l, jnp.concatenate([k_c, k_l], axis=1), jnp.concatenate([v_c, v_l], axis=1))
    hrec_l, _, _ = rglru_bidir(xb_l, lp, s_fwd, s_bwd)
    branches_l = [attn_l, conformer_branch(glu_l, lp), fnet_branch(f_l), hrec_l * jax.nn.gelu(gb_l)]
    out_l = merge_branches(branches_l, gl_l, lp['w_branch'], lp['w_out'])
    new_lat = post_norm(x_lat, out_l, gate_l, lp['ln1_g'], lp['ln1_b'])
    if not ctx_out:
        return new_lat, None

    cq_c, glu_c, f_c, gb_c, gl_c = p_c[3], p_c[4], p_c[5], p_c[6], p_c[7]
    attn_c = attend(mla_queries(cq_c, lp['q_norm_g'], lp['w_uq'], None), k_c, v_c)
    branches_c = [attn_c, conformer_branch(glu_c, lp), fnet_branch(f_c), hrec_c * jax.nn.gelu(gb_c)]
    out_c = merge_branches(branches_c, gl_c, lp['w_branch'], lp['w_out'])
    new_ctx = post_norm(x_ctx, out_c, mod_c[2], lp['ln1_g'], lp['ln1_b'])
    return new_lat, new_ctx


def ec_moe(h, w_router, w_e_gate, w_e_up, w_e_down):
    B, N, _ = h.shape
    cap = EC_CAPACITY * N // N_EXPERTS
    affinity = jax.nn.softmax(h @ w_router, axis=-1)
    top_w, top_i = lax.top_k(jnp.swapaxes(affinity, 1, 2), cap)
    b_idx = jnp.arange(B)[:, None, None]
    xe = h[b_idx, top_i]
    he = jax.nn.silu(jnp.einsum('becd,edf->becf', xe, w_e_gate)) * jnp.einsum('becd,edf->becf', xe, w_e_up)
    ye = jnp.einsum('becf,efd->becd', he, w_e_down) * top_w[..., None]
    return jnp.zeros_like(h).at[b_idx, top_i].add(ye)


def moe_sublayer(x, shift, scale, gate, lp):
    h = modulate(x, shift, scale)
    out = ec_moe(h, lp['w_router'], lp['w_e_gate'], lp['w_e_up'], lp['w_e_down'])
    return post_norm(x, out, gate, lp['ln2_g'], lp['ln2_b'])


def kernel(x, c, ctx, c_ctx, ada_w, ada_b, w_in, q_norm_g, w_uq, kv_norm_g, w_ukv, cv_w, cv_b, cv_ln_g, cv_ln_b, lru_conv_w, lru_conv_b, lru_wa, lru_ba, lru_wx, lru_bx, lru_lambda, w_branch, w_out, ln1_g, ln1_b, w_router, w_e_gate, w_e_up, w_e_down, ln2_g, ln2_b):
    rows = x.shape[1] // GRID_W
    tabs = rope_tables(rows)
    x_lat, x_ctx = x, ctx
    for l in range(DEPTH):
        ctx_out = l < DEPTH - 1
        lp = {
            'w_in': w_in[l], 'q_norm_g': q_norm_g[l], 'w_uq': w_uq[l], 'kv_norm_g': kv_norm_g[l],
            'w_ukv': w_ukv[l], 'cv_w': cv_w[l], 'cv_b': cv_b[l], 'cv_ln_g': cv_ln_g[l], 'cv_ln_b': cv_ln_b[l],
            'lru_conv_w': lru_conv_w[l], 'lru_conv_b': lru_conv_b[l], 'lru_wa': lru_wa[l], 'lru_ba': lru_ba[l],
            'lru_wx': lru_wx[l], 'lru_bx': lru_bx[l], 'lru_lambda': lru_lambda[l], 'w_branch': w_branch[l],
            'w_out': w_out[l], 'ln1_g': ln1_g[l], 'ln1_b': ln1_b[l], 'w_router': w_router[l],
            'w_e_gate': w_e_gate[l], 'w_e_up': w_e_up[l], 'w_e_down': w_e_down[l],
            'ln2_g': ln2_g[l], 'ln2_b': ln2_b[l],
        }
        mod = (jax.nn.silu(c) @ ada_w[l] + ada_b[l])[:, None, :]
        shift1, scale1, gate1, shift2, scale2, gate2 = jnp.split(mod, 6, axis=-1)
        n_mod_c = 6 if ctx_out else 2
        mod_c = jax.nn.silu(c_ctx) @ ada_w[l][:, :n_mod_c * D_MODEL] + ada_b[l][:n_mod_c * D_MODEL]
        mods_c = jnp.split(mod_c, n_mod_c)
        x_lat, x_ctx_new = mixer_sublayer(x_lat, x_ctx, (shift1, scale1, gate1), mods_c, lp, tabs, ctx_out)
        x_lat = moe_sublayer(x_lat, shift2, scale2, gate2, lp)
        if ctx_out:
            x_ctx = moe_sublayer(x_ctx_new, mods_c[3], mods_c[4], mods_c[5], lp)
    return x_lat
```

```python
import functools
import math

import jax
import jax.numpy as jnp
from jax import lax
from jax.experimental import pallas as pl
from jax.experimental.pallas import tpu as pltpu

D_MODEL = 1024
DEPTH = 4
GRID_W = 64
MLA_HEADS = 8
Q_LORA = 384
KV_LORA = 256
QK_NOPE = 64
QK_ROPE = 32
V_DIM = 64
ROPE_AXIS_FREQ = QK_ROPE // 4
ROPE_BASE = 10000.0
CONV_WIDTH = 512
CONV_K = 31
FNET_WIDTH = 512
FNET_GROUPS = 4
LRU_WIDTH = 512
LRU_HEADS = 8
LRU_CONV_K = 4
LRU_C = 8.0
N_BRANCH = 4
BRANCH_WIDTH = 512
N_EXPERTS = 16
EXPERT_FF = 1408
EC_CAPACITY = 2
DEEPNORM_ALPHA = (2 * DEPTH) ** 0.25
LN_EPS = 1e-6

V7X_LANES = 128
V7X_SUBLANES = 8
V7X_VMEM_LIMIT_BYTES = 56 * 1024 * 1024
HEAD_SLOT = V7X_LANES
TOKEN_TILE = 256
KVIN_WIDTH = KV_LORA + V7X_LANES
IN_PROJ_WIDTH = KVIN_WIDTH + LRU_WIDTH + Q_LORA + 2 * CONV_WIDTH + FNET_WIDTH + LRU_WIDTH

F32 = jnp.float32
BF16 = jnp.bfloat16


def _params(*sem):
    return pltpu.CompilerParams(dimension_semantics=sem, vmem_limit_bytes=V7X_VMEM_LIMIT_BYTES)


def _sigmoid(x):
    return 1.0 / (1.0 + jnp.exp(-x))


def _ln(x):
    mu = jnp.mean(x, axis=-1, keepdims=True)
    xc = x - mu
    var = jnp.mean(xc * xc, axis=-1, keepdims=True)
    return xc * lax.rsqrt(var + LN_EPS)


def _rms(x, g):
    return x * lax.rsqrt(jnp.mean(x * x, axis=-1, keepdims=True) + LN_EPS) * g


def _dot(a, b):
    return jnp.dot(a, b, preferred_element_type=F32)


def _mod_spec(mods, tiles_per_mod):
    if mods.shape[0] == 1:
        return pl.BlockSpec((1, 1, D_MODEL), lambda i: (0, 0, 0))
    return pl.BlockSpec((1, 1, D_MODEL), lambda i: (i // tiles_per_mod, 0, 0))


def _full_spec(arr):
    zeros = (0,) * arr.ndim
    return pl.BlockSpec(arr.shape, lambda *_: zeros)


def _in_proj_body(x_ref, sh_ref, sc_ref, w_ref, wcs_ref,
                  kvin_ref, xb_ref, cq_ref, u_ref, fa_ref, fb_ref, gb_ref):
    h = _ln(x_ref[...]) * (1.0 + sc_ref[0]) + sh_ref[0]
    p = _dot(h.astype(BF16), w_ref[...])
    o = 0
    kvin_ref[...] = p[:, o:o + KVIN_WIDTH]
    o += KVIN_WIDTH
    xb_ref[...] = p[:, o:o + LRU_WIDTH]
    o += LRU_WIDTH
    cq_ref[...] = p[:, o:o + Q_LORA]
    o += Q_LORA
    a = p[:, o:o + CONV_WIDTH]
    g = p[:, o + CONV_WIDTH:o + 2 * CONV_WIDTH]
    u_ref[...] = a * _sigmoid(g)
    o += 2 * CONV_WIDTH
    f = p[:, o:o + FNET_WIDTH].astype(BF16)
    o += FNET_WIDTH
    gb_ref[...] = p[:, o:o + LRU_WIDTH]
    ab = _dot(f, wcs_ref[...])
    fa_ref[...] = ab[:, :FNET_WIDTH].astype(BF16)
    fb_ref[...] = ab[:, FNET_WIDTH:].astype(BF16)


def in_proj(x, shift, scale, w1, wcs, tiles_per_mod):
    M = x.shape[0]
    tm = TOKEN_TILE
    row = lambda i: (i, 0)
    widths = (KVIN_WIDTH, LRU_WIDTH, Q_LORA, CONV_WIDTH, FNET_WIDTH, FNET_WIDTH, LRU_WIDTH)
    dtypes = (F32, F32, F32, F32, BF16, BF16, F32)
    return pl.pallas_call(
        _in_proj_body,
        grid=(M // tm,),
        in_specs=[pl.BlockSpec((tm, D_MODEL), row), _mod_spec(shift, tiles_per_mod),
                  _mod_spec(scale, tiles_per_mod), _full_spec(w1), _full_spec(wcs)],
        out_specs=[pl.BlockSpec((tm, w), row) for w in widths],
        out_shape=[jax.ShapeDtypeStruct((M, w), d) for w, d in zip(widths, dtypes)],
        compiler_params=_params("arbitrary"),
        name="in_proj",
    )(x, shift, scale, w1, wcs)


def _qkv_body(kvin_ref, cq_ref, gq_ref, gkv_ref, wq1_ref, wq2_ref, wkn_ref, wkr_ref, wv_ref,
              qc_ref, qs_ref, kcs_ref, q_ref, k_ref, v_ref):
    nq = _rms(cq_ref[...], gq_ref[...]).astype(BF16)
    q1 = _dot(nq, wq1_ref[...])
    q2 = _dot(nq, wq2_ref[...])
    kvin = kvin_ref[...]
    n = _rms(kvin[:, :KV_LORA], gkv_ref[...]).astype(BF16)
    kr = (kvin[:, KV_LORA:] * kcs_ref[...]).astype(BF16)
    kk = _dot(n, wkn_ref[...]) + _dot(kr, wkr_ref[...])
    v = _dot(n, wv_ref[...])
    qc = qc_ref[...]
    qs = qs_ref[...]
    scale = (QK_NOPE + QK_ROPE) ** -0.5
    for h in range(MLA_HEADS):
        sl = slice(h * HEAD_SLOT, (h + 1) * HEAD_SLOT)
        q_ref[0, h] = ((q1[:, sl] * qc + q2[:, sl] * qs) * scale).astype(BF16)
        k_ref[0, h] = kk[:, sl].astype(BF16)
    for p in range(MLA_HEADS // 2):
        v_ref[0, p] = v[:, p * V7X_LANES:(p + 1) * V7X_LANES].astype(BF16)


def qkv_proj(kvin, cq, gq, gkv, wq1, wq2, wkn, wkr, wv, qc, qs, kcs, B, S):
    tm = TOKEN_TILE
    tps = S // tm
    row = lambda i: (i, 0)
    tab = lambda i: (i % tps, 0)
    seq = lambda i: (i // tps, 0, i % tps, 0)
    return pl.pallas_call(
        _qkv_body,
        grid=(B * tps,),
        in_specs=[pl.BlockSpec((tm, KVIN_WIDTH), row), pl.BlockSpec((tm, Q_LORA), row),
                  _full_spec(gq), _full_spec(gkv), _full_spec(wq1), _full_spec(wq2),
                  _full_spec(wkn), _full_spec(wkr), _full_spec(wv),
                  pl.BlockSpec((tm, HEAD_SLOT), tab), pl.BlockSpec((tm, HEAD_SLOT), tab),
                  pl.BlockSpec((tm, V7X_LANES), tab)],
        out_specs=[pl.BlockSpec((1, MLA_HEADS, tm, HEAD_SLOT), seq),
                   pl.BlockSpec((1, MLA_HEADS, tm, HEAD_SLOT), seq),
                   pl.BlockSpec((1, MLA_HEADS // 2, tm, V7X_LANES), seq)],
        out_shape=[jax.ShapeDtypeStruct((B, MLA_HEADS, S, HEAD_SLOT), BF16),
                   jax.ShapeDtypeStruct((B, MLA_HEADS, S, HEAD_SLOT), BF16),
                   jax.ShapeDtypeStruct((B, MLA_HEADS // 2, S, V7X_LANES), BF16)],
        compiler_params=_params("arbitrary"),
        name="qkv_proj",
    )(kvin, cq, gq, gkv, wq1, wq2, wkn, wkr, wv, qc, qs, kcs)


def _attn_body(*refs, n_sets):
    q_ref = refs[0]
    k_refs = refs[1:1 + 2 * n_sets:2]
    v_refs = refs[2:2 + 2 * n_sets:2]
    o_ref = refs[-1]
    outs = []
    for hh in range(2):
        q = q_ref[0, hh]
        s = [lax.dot_general(q, k_ref[0, hh], (((1,), (1,)), ((), ())), preferred_element_type=F32)
             for k_ref in k_refs]
        m = functools.reduce(jnp.maximum, [jnp.max(si, axis=-1, keepdims=True) for si in s])
        p = [jnp.exp(si - m) for si in s]
        l = functools.reduce(jnp.add, [jnp.sum(pi, axis=-1, keepdims=True) for pi in p])
        o = functools.reduce(jnp.add, [_dot(pi.astype(BF16), v_ref[0, 0]) for pi, v_ref in zip(p, v_refs)])
        outs.append(o / l)
    lane = lax.broadcasted_iota(jnp.int32, outs[0].shape, 1)
    o_ref[0] = jnp.where(lane < V_DIM, outs[0], outs[1]).astype(BF16)


def attention(q, kv_sets):
    B, H, S, _ = q.shape
    tq = TOKEN_TILE
    in_specs = [pl.BlockSpec((1, 2, tq, HEAD_SLOT), lambda b, hp, i: (b, hp, i, 0))]
    args = [q]
    for k, v in kv_sets:
        sk = k.shape[2]
        in_specs.append(pl.BlockSpec((1, 2, sk, HEAD_SLOT), lambda b, hp, i: (b, hp, 0, 0)))
        in_specs.append(pl.BlockSpec((1, 1, sk, V7X_LANES), lambda b, hp, i: (b, hp, 0, 0)))
        args += [k, v]
    return pl.pallas_call(
        functools.partial(_attn_body, n_sets=len(kv_sets)),
        grid=(B, H // 2, S // tq),
        in_specs=in_specs,
        out_specs=pl.BlockSpec((1, tq, V7X_LANES), lambda b, hp, i: (b, i, hp)),
        out_shape=jax.ShapeDtypeStruct((B, S, H * V_DIM), BF16),
        compiler_params=_params("arbitrary", "arbitrary", "arbitrary"),
        name="attention",
    )(*args)


def _fill_window(win_ref, x_ref, j, n_chunks, tc, halo):
    c = win_ref.shape[1]
    start = pl.multiple_of(j * tc, tc)
    win_ref[halo:halo + tc, :] = x_ref[0, pl.ds(start, tc), :]
    lo = pl.multiple_of(jnp.maximum(j * tc - halo, 0), halo)
    prev = x_ref[0, pl.ds(lo, halo), :]
    win_ref[0:halo, :] = jnp.where(j > 0, prev, jnp.zeros((halo, c), F32))
    hi = pl.multiple_of(jnp.minimum((j + 1) * tc, (n_chunks - 1) * tc + tc - halo), halo)
    nxt = x_ref[0, pl.ds(hi, halo), :]
    win_ref[halo + tc:2 * halo + tc, :] = jnp.where(j < n_chunks - 1, nxt, jnp.zeros((halo, c), F32))


def _conv_taps(win_ref, w_ref, b_ref, out_ref, tc, first_row, n_taps, row_block=64):
    c = win_ref.shape[1]
    for r0 in range(0, tc, row_block):
        for l0 in range(0, c, V7X_LANES):
            ls = slice(l0, l0 + V7X_LANES)
            acc = jnp.broadcast_to(b_ref[:, ls], (row_block, V7X_LANES))
            for k in range(n_taps):
                r = first_row + r0 + k
                acc = acc + win_ref[r:r + row_block, ls] * w_ref[k:k + 1, ls]
            out_ref[r0:r0 + row_block, ls] = acc


CONF_HALO = 16


def _conf_body(u_ref, w_ref, b_ref, g_ref, beta_ref, o_ref, win_ref, acc_ref, *, n_chunks, tc):
    j = pl.program_id(1)
    _fill_window(win_ref, u_ref, j, n_chunks, tc, CONF_HALO)
    _conv_taps(win_ref, w_ref, b_ref, acc_ref, tc, CONF_HALO - (CONV_K - 1) // 2, CONV_K)
    y = _ln(acc_ref[...]) * g_ref[...] + beta_ref[...]
    o_ref[0] = (y * _sigmoid(y)).astype(BF16)


def conformer(u, w, b, g, beta):
    B, S, C = u.shape
    tc = TOKEN_TILE
    n_chunks = S // tc
    return pl.pallas_call(
        functools.partial(_conf_body, n_chunks=n_chunks, tc=tc),
        grid=(B, n_chunks),
        in_specs=[pl.BlockSpec((1, S, C), lambda bb, j: (bb, 0, 0)),
                  _full_spec(w), _full_spec(b), _full_spec(g), _full_spec(beta)],
        out_specs=pl.BlockSpec((1, tc, C), lambda bb, j: (bb, j, 0)),
        out_shape=jax.ShapeDtypeStruct((B, S, C), BF16),
        scratch_shapes=[pltpu.VMEM((tc + 2 * CONF_HALO, C), F32), pltpu.VMEM((tc, C), F32)],
        compiler_params=_params("arbitrary", "arbitrary"),
        name="conformer",
    )(u, w, b, g, beta)


def _fnet_body(c_ref, s_ref, a_ref, b_ref, o_ref, *, norm):
    y = _dot(c_ref[...], a_ref[0]) + _dot(s_ref[...], b_ref[0])
    o_ref[0] = (y * norm).astype(BF16)


def fnet_seq_dft(cmat, smat, fa, fb):
    B, S, W = fa.shape
    ts = min(S, 512)
    norm = 1.0 / math.sqrt(S * (FNET_WIDTH // FNET_GROUPS))
    return pl.pallas_call(
        functools.partial(_fnet_body, norm=norm),
        grid=(S // ts, B),
        in_specs=[pl.BlockSpec((ts, S), lambda i, b: (i, 0)), pl.BlockSpec((ts, S), lambda i, b: (i, 0)),
                  pl.BlockSpec((1, S, W), lambda i, b: (b, 0, 0)), pl.BlockSpec((1, S, W), lambda i, b: (b, 0, 0))],
        out_specs=pl.BlockSpec((1, ts, W), lambda i, b: (b, i, 0)),
        out_shape=jax.ShapeDtypeStruct((B, S, W), BF16),
        compiler_params=_params("arbitrary", "arbitrary"),
        name="fnet_seq_dft",
    )(cmat, smat, fa, fb)


LRU_HALO = 8
LRU_GROUP = V7X_SUBLANES


def _lru_body(*refs, n_chunks, tc, reverse):
    if reverse:
        (x_ref, cw_ref, cb_ref, wg_ref, bg_ref, ca_ref, h0_ref, hf_ref, gb_ref,
         o_ref, s_ref, win_ref, xc_ref, a_ref, u_ref, carry_ref) = refs
    else:
        (x_ref, cw_ref, cb_ref, wg_ref, bg_ref, ca_ref, h0_ref,
         o_ref, s_ref, win_ref, xc_ref, a_ref, u_ref, carry_ref) = refs
    step = pl.program_id(1)
    j = (n_chunks - 1 - step) if reverse else step

    @pl.when(step == 0)
    def _():
        carry_ref[...] = jnp.broadcast_to(h0_ref[0], carry_ref.shape)

    _fill_window(win_ref, x_ref, j, n_chunks, tc, LRU_HALO)
    _conv_taps(win_ref, cw_ref, cb_ref, xc_ref, tc, LRU_HALO - LRU_CONV_K // 2, LRU_CONV_K)
    xc = xc_ref[...]
    gates = _dot(xc.astype(BF16), wg_ref[...]) + bg_ref[...]
    r = _sigmoid(gates[:, :LRU_WIDTH])
    i = _sigmoid(gates[:, LRU_WIDTH:])
    log_a = ca_ref[...] * r
    a = jnp.exp(log_a)
    a_ref[...] = a
    u_ref[...] = jnp.sqrt(-jnp.tanh(log_a) * (a * a + 1.0)) * (i * xc)

    n_groups = tc // LRU_GROUP
    row = lax.broadcasted_iota(jnp.int32, (LRU_GROUP, LRU_WIDTH), 0)

    def group(gi, carry):
        g = (n_groups - 1 - gi) if reverse else gi
        rows = pl.ds(pl.multiple_of(g * LRU_GROUP, LRU_GROUP), LRU_GROUP)
        a = a_ref[rows, :]
        u = u_ref[rows, :]
        for sh in (1, 2, 4):
            if reverse:
                a_sh = pltpu.roll(a, LRU_GROUP - sh, 0)
                u_sh = pltpu.roll(u, LRU_GROUP - sh, 0)
                valid = row < LRU_GROUP - sh
            else:
                a_sh = pltpu.roll(a, sh, 0)
                u_sh = pltpu.roll(u, sh, 0)
                valid = row >= sh
            a_sh = jnp.where(valid, a_sh, 1.0)
            u_sh = jnp.where(valid, u_sh, 0.0)
            u = a * u_sh + u
            a = a * a_sh
        h = u + a * carry
        u_ref[rows, :] = h
        last = h[0:1] if reverse else h[LRU_GROUP - 1:LRU_GROUP]
        return jnp.broadcast_to(last, h.shape)

    carry = lax.fori_loop(0, n_groups, group, carry_ref[...])
    carry_ref[...] = carry
    h = u_ref[...]
    if reverse:
        gb = gb_ref[0]
        gelu = 0.5 * gb * (1.0 + jnp.tanh(math.sqrt(2.0 / math.pi) * (gb + 0.044715 * (gb * gb * gb))))
        o_ref[0] = ((hf_ref[0] + h) * gelu).astype(BF16)
    else:
        o_ref[0] = h

    @pl.when(step == n_chunks - 1)
    def _():
        s_ref[0] = carry[0:1]


def lru_scan(xb, cw, cb, wg, bg, ca, h0, hf=None, gb=None):
    B, S, W = xb.shape
    reverse = hf is not None
    tc = TOKEN_TILE
    n_chunks = S // tc
    chunk = (lambda bb, s: (bb, n_chunks - 1 - s, 0)) if reverse else (lambda bb, s: (bb, s, 0))
    per_b = lambda bb, s: (bb, 0, 0)
    in_specs = [pl.BlockSpec((1, S, W), per_b), _full_spec(cw), _full_spec(cb), _full_spec(wg),
                _full_spec(bg), _full_spec(ca), pl.BlockSpec((1, 1, W), per_b)]
    args = [xb, cw, cb, wg, bg, ca, h0]
    if reverse:
        in_specs += [pl.BlockSpec((1, tc, W), chunk), pl.BlockSpec((1, tc, W), chunk)]
        args += [hf, gb]
    return pl.pallas_call(
        functools.partial(_lru_body, n_chunks=n_chunks, tc=tc, reverse=reverse),
        grid=(B, n_chunks),
        in_specs=in_specs,
        out_specs=[pl.BlockSpec((1, tc, W), chunk), pl.BlockSpec((1, 1, W), per_b)],
        out_shape=[jax.ShapeDtypeStruct((B, S, W), BF16 if reverse else F32),
                   jax.ShapeDtypeStruct((B, 1, W), F32)],
        scratch_shapes=[pltpu.VMEM((tc + 2 * LRU_HALO, W), F32), pltpu.VMEM((tc, W), F32),
                        pltpu.VMEM((tc, W), F32), pltpu.VMEM((tc, W), F32),
                        pltpu.VMEM((LRU_GROUP, W), F32)],
        compiler_params=_params("arbitrary", "arbitrary"),
        name="lru_bwd" if reverse else "lru_fwd",
    )(*args)


def _split_bf16(x):
    hi = x.astype(BF16)
    return hi, (x - hi.astype(F32)).astype(BF16)


def _merge_body(x_ref, sh1_ref, sc1_ref, g1_ref, sh2_ref, sc2_ref, attn_ref, conf_ref, fnet_ref, lru_ref,
                wgl_ref, wb_ref, wout_ref, lng_ref, lnb_ref, wrh_ref, wrl_ref,
                y_ref, h2_ref, aff_ref):
    x = x_ref[...]
    h = (_ln(x) * (1.0 + sc1_ref[0]) + sh1_ref[0]).astype(BF16)
    acc = None
    for k, br_ref in enumerate((attn_ref, conf_ref, fnet_ref, lru_ref)):
        gl = _dot(h, wgl_ref[:, k * D_MODEL:(k + 1) * D_MODEL])
        term = _sigmoid(gl) * _dot(br_ref[...], wb_ref[k])
        acc = term if acc is None else acc + term
    out = _dot(acc.astype(BF16), wout_ref[...])
    y = _ln(DEEPNORM_ALPHA * x + g1_ref[0] * out) * lng_ref[...] + lnb_ref[...]
    y_ref[...] = y
    h2 = _ln(y) * (1.0 + sc2_ref[0]) + sh2_ref[0]
    h2_ref[...] = h2.astype(BF16)
    hh, hl = _split_bf16(h2)
    logits = _dot(hh, wrh_ref[...]) + _dot(hh, wrl_ref[...]) + _dot(hl, wrh_ref[...])
    lane = lax.broadcasted_iota(jnp.int32, logits.shape, 1)
    valid = lane < N_EXPERTS
    logits = jnp.where(valid, logits, -jnp.inf)
    e = jnp.exp(logits - jnp.max(logits, axis=-1, keepdims=True))
    aff_ref[...] = e / jnp.sum(e, axis=-1, keepdims=True)


def merge(x, mods, branches, wgl, wb, wout, lng, lnb, wrh, wrl, tiles_per_mod):
    M = x.shape[0]
    tm = TOKEN_TILE
    row = lambda i: (i, 0)
    sh1, sc1, g1, sh2, sc2 = mods
    in_specs = ([pl.BlockSpec((tm, D_MODEL), row)] + [_mod_spec(m, tiles_per_mod) for m in mods]
                + [pl.BlockSpec((tm, BRANCH_WIDTH), row)] * N_BRANCH
                + [_full_spec(a) for a in (wgl, wb, wout, lng, lnb, wrh, wrl)])
    return pl.pallas_call(
        _merge_body,
        grid=(M // tm,),
        in_specs=in_specs,
        out_specs=[pl.BlockSpec((tm, D_MODEL), row), pl.BlockSpec((tm, D_MODEL), row),
                   pl.BlockSpec((tm, V7X_LANES), row)],
        out_shape=[jax.ShapeDtypeStruct((M, D_MODEL), F32), jax.ShapeDtypeStruct((M, D_MODEL), BF16),
                   jax.ShapeDtypeStruct((M, V7X_LANES), F32)],
        compiler_params=_params("arbitrary"),
        name="merge",
    )(x, sh1, sc1, g1, sh2, sc2, *branches, wgl, wb, wout, lng, lnb, wrh, wrl)


def _select_body(aff_ref, tri_ref, pos_ref, *, cap, n_tok):
    bits = pltpu.bitcast(aff_ref[0], jnp.int32)

    def bisect(it, lo):
        cand = lo | jnp.left_shift(jnp.int32(1), 30 - it)
        cnt = jnp.sum(jnp.where(bits >= cand, 1.0, 0.0), axis=1, keepdims=True)
        return jnp.where(cnt >= cap, cand, lo)

    thr = lax.fori_loop(0, 31, bisect, jnp.zeros((N_EXPERTS, 1), jnp.int32))
    need = cap - jnp.sum(jnp.where(bits > thr, 1.0, 0.0), axis=1, keepdims=True)
    tri = tri_ref[...]
    ties_before = jnp.zeros((N_EXPERTS, 1), F32)
    sel_before = jnp.zeros((N_EXPERTS, 1), F32)
    for c0 in range(0, n_tok, V7X_LANES):
        sl = slice(c0, c0 + V7X_LANES)
        bits_c = bits[:, sl]
        gt_c = jnp.where(bits_c > thr, 1.0, 0.0)
        eq_c = jnp.where(bits_c == thr, 1.0, 0.0)
        tie_rank = _dot(eq_c.astype(BF16), tri) + ties_before
        ties_before = ties_before + jnp.sum(eq_c, axis=1, keepdims=True)
        sel_c = gt_c + eq_c * jnp.where(tie_rank < need, 1.0, 0.0)
        slot = _dot(sel_c.astype(BF16), tri) + sel_before
        sel_before = sel_before + jnp.sum(sel_c, axis=1, keepdims=True)
        pos_ref[0, :, sl] = jnp.where(sel_c > 0.0, slot, -1.0)


def select_tokens(aff_t, tri, cap):
    B, E, N = aff_t.shape
    return pl.pallas_call(
        functools.partial(_select_body, cap=cap, n_tok=N),
        grid=(B,),
        in_specs=[pl.BlockSpec((1, E, N), lambda b: (b, 0, 0)), _full_spec(tri)],
        out_specs=pl.BlockSpec((1, E, N), lambda b: (b, 0, 0)),
        out_shape=jax.ShapeDtypeStruct((B, E, N), F32),
        compiler_params=_params("arbitrary"),
        name="select_tokens",
    )(aff_t, tri)


def _experts_body(pos_ref, h_ref, wg_ref, wu_ref, wd_ref, y_ref, *, rows):
    c = pl.program_id(2)
    slot = lax.broadcasted_iota(jnp.int32, (rows, 1), 0) + c * rows
    pos = pos_ref[0].astype(jnp.int32)
    onehot = jnp.where(slot == pos, 1.0, 0.0).astype(BF16)
    xe = _dot(onehot, h_ref[...]).astype(BF16)
    g = _dot(xe, wg_ref[0])
    u = _dot(xe, wu_ref[0])
    he = (g * _sigmoid(g) * u).astype(BF16)
    y_ref[0, 0] = _dot(he, wd_ref[0]).astype(BF16)


def experts(pos_rows, h2, wg, wu, wd, B, N, cap):
    rows = min(cap, 256)
    n_c = cap // rows
    E = N_EXPERTS
    return pl.pallas_call(
        functools.partial(_experts_body, rows=rows),
        grid=(E, B, n_c),
        in_specs=[pl.BlockSpec((1, 1, N), lambda e, b, c: (b * E + e, 0, 0)),
                  pl.BlockSpec((N, D_MODEL), lambda e, b, c: (b, 0)),
                  pl.BlockSpec((1, D_MODEL, EXPERT_FF), lambda e, b, c: (e, 0, 0)),
                  pl.BlockSpec((1, D_MODEL, EXPERT_FF), lambda e, b, c: (e, 0, 0)),
                  pl.BlockSpec((1, EXPERT_FF, D_MODEL), lambda e, b, c: (e, 0, 0))],
        out_specs=pl.BlockSpec((1, 1, rows, D_MODEL), lambda e, b, c: (b, e, c, 0)),
        out_shape=jax.ShapeDtypeStruct((B, E, cap, D_MODEL), BF16),
        compiler_params=_params("arbitrary", "arbitrary", "arbitrary"),
        name="experts",
    )(pos_rows, h2, wg, wu, wd)


def _combine_body(x_ref, g2_ref, posc_ref, aff_ref, ye_ref, lng_ref, lnb_ref, o_ref, *, cap):
    posc = posc_ref[0]
    aff = aff_ref[...]
    slot = lax.broadcasted_iota(jnp.int32, (1, cap), 1).astype(F32)
    acc = None
    for e in range(N_EXPERTS):
        pcol = posc[:, e:e + 1]
        onehot = jnp.where(pcol == slot, 1.0, 0.0).astype(BF16)
        w = jnp.where(pcol >= 0.0, aff[:, e:e + 1], 0.0)
        term = w * _dot(onehot, ye_ref[0, e])
        acc = term if acc is None else acc + term
    y = _ln(DEEPNORM_ALPHA * x_ref[...] + g2_ref[0] * acc)
    o_ref[...] = y * lng_ref[...] + lnb_ref[...]


def combine(x, gate2, posc, aff, ye, lng, lnb, B, N, cap):
    tm = min(N, 512)
    tpb = N // tm
    row = lambda i: (i, 0)
    gate_spec = (pl.BlockSpec((1, 1, D_MODEL), lambda i: (0, 0, 0)) if gate2.shape[0] == 1
                 else pl.BlockSpec((1, 1, D_MODEL), lambda i: (i // tpb, 0, 0)))
    return pl.pallas_call(
        functools.partial(_combine_body, cap=cap),
        grid=(B * tpb,),
        in_specs=[pl.BlockSpec((tm, D_MODEL), row), gate_spec,
                  pl.BlockSpec((1, tm, V7X_LANES), lambda i: (i // tpb, i % tpb, 0)),
                  pl.BlockSpec((tm, V7X_LANES), row),
                  pl.BlockSpec((1, N_EXPERTS, cap, D_MODEL), lambda i: (i // tpb, 0, 0, 0)),
                  _full_spec(lng), _full_spec(lnb)],
        out_specs=pl.BlockSpec((tm, D_MODEL), row),
        out_shape=jax.ShapeDtypeStruct((B * N, D_MODEL), F32),
        compiler_params=_params("arbitrary"),
        name="combine",
    )(x, gate2, posc, aff, ye, lng, lnb)


def _rope_rotation():
    r = [[0.0] * QK_ROPE for _ in range(QK_ROPE)]
    half = QK_ROPE // 4
    for base in (0, QK_ROPE // 2):
        for j in range(half):
            r[base + j + half][base + j] = -1.0
            r[base + j][base + j + half] = 1.0
    return jnp.array(r, F32)


def _rope_tables(rows):
    row = jnp.repeat(jnp.arange(rows), GRID_W).astype(F32)
    col = jnp.tile(jnp.arange(GRID_W), rows).astype(F32)
    inv = ROPE_BASE ** (-jnp.arange(ROPE_AXIS_FREQ, dtype=F32) / ROPE_AXIS_FREQ)
    ang_r = row[:, None] * inv
    ang_c = col[:, None] * inv
    cos = jnp.concatenate([jnp.cos(ang_r)] * 2 + [jnp.cos(ang_c)] * 2, axis=-1)
    sin = jnp.concatenate([jnp.sin(ang_r)] * 2 + [jnp.sin(ang_c)] * 2, axis=-1)
    return cos, sin


def _position_tables(cos, sin):
    S = cos.shape[0]
    ones = jnp.ones((S, QK_NOPE), F32)
    pad = jnp.zeros((S, HEAD_SLOT - QK_NOPE - QK_ROPE), F32)
    qc = jnp.concatenate([ones, cos, pad], axis=-1)
    qs = jnp.concatenate([jnp.zeros((S, QK_NOPE), F32), sin, pad], axis=-1)
    kcs = jnp.concatenate([cos, sin, jnp.zeros((S, V7X_LANES - 2 * QK_ROPE), F32)], axis=-1)
    return qc, qs, kcs


def _dft_matrices(S):
    k = jnp.arange(S, dtype=jnp.int32)
    ang = ((k[:, None] * k[None, :]) % S).astype(F32) * (2.0 * math.pi / S)
    return jnp.cos(ang).astype(BF16), (-jnp.sin(ang)).astype(BF16)


def _channel_dft_weights():
    n = FNET_WIDTH // FNET_GROUPS
    k = jnp.arange(n, dtype=jnp.int32)
    ang = ((k[:, None] * k[None, :]) % n).astype(F32) * (2.0 * math.pi / n)
    eye = jnp.eye(FNET_GROUPS, dtype=F32)
    return jnp.concatenate([jnp.kron(eye, jnp.cos(ang)), jnp.kron(eye, jnp.sin(ang))], axis=1).astype(BF16)


def _block_diag_heads(w):
    H, d, _ = w.shape
    out = jnp.zeros((H, d, H, d), w.dtype)
    out = out.at[jnp.arange(H), :, jnp.arange(H), :].set(w)
    return out.reshape(H * d, H * d)


def _layer_weights(l, w_in, q_norm_g, w_uq, kv_norm_g, w_ukv, lru_wa, lru_ba, lru_wx, lru_bx, lru_lambda,
                   w_branch, w_out, w_router, rot):
    wi = w_in[l]
    o = 0
    ckv = wi[:, o:o + KV_LORA]
    o += KV_LORA
    kr = wi[:, o:o + QK_ROPE]
    o += QK_ROPE
    xb = wi[:, o:o + LRU_WIDTH]
    o += LRU_WIDTH
    cq = wi[:, o:o + Q_LORA]
    o += Q_LORA
    glu = wi[:, o:o + 2 * CONV_WIDTH]
    o += 2 * CONV_WIDTH
    f = wi[:, o:o + FNET_WIDTH]
    o += FNET_WIDTH
    gb = wi[:, o:o + LRU_WIDTH]
    o += LRU_WIDTH
    gl = wi[:, o:]
    kpad = jnp.zeros((D_MODEL, V7X_LANES - 2 * QK_ROPE), F32)
    w1 = jnp.concatenate([ckv, kr, kr @ rot, kpad, xb, cq, glu, f, gb], axis=1).astype(BF16)

    uq = w_uq[l]
    zq = jnp.zeros((Q_LORA, MLA_HEADS, HEAD_SLOT - QK_NOPE - QK_ROPE), F32)
    wq1 = jnp.concatenate([uq, zq], axis=-1).reshape(Q_LORA, MLA_HEADS * HEAD_SLOT).astype(BF16)
    uq_rot = jnp.einsum('rhd,de->rhe', uq[..., QK_NOPE:], rot)
    wq2 = jnp.concatenate([jnp.zeros((Q_LORA, MLA_HEADS, QK_NOPE), F32), uq_rot, zq], axis=-1)
    wq2 = wq2.reshape(Q_LORA, MLA_HEADS * HEAD_SLOT).astype(BF16)
    ukv = w_ukv[l]
    zk = jnp.zeros((KV_LORA, MLA_HEADS, HEAD_SLOT - QK_NOPE), F32)
    wkn = jnp.concatenate([ukv[..., :QK_NOPE], zk], axis=-1).reshape(KV_LORA, MLA_HEADS * HEAD_SLOT).astype(BF16)
    wv = ukv[..., QK_NOPE:].reshape(KV_LORA, MLA_HEADS * V_DIM).astype(BF16)
    eye = jnp.eye(QK_ROPE, dtype=F32)
    slot = jnp.concatenate([jnp.zeros((QK_ROPE, QK_NOPE), F32), eye,
                            jnp.zeros((QK_ROPE, HEAD_SLOT - QK_NOPE - QK_ROPE), F32)], axis=1)
    slot = jnp.tile(slot, (1, MLA_HEADS))
    wkr = jnp.concatenate([slot, slot, jnp.zeros((V7X_LANES - 2 * QK_ROPE, MLA_HEADS * HEAD_SLOT), F32)],
                          axis=0).astype(BF16)

    lru = []
    for d in range(2):
        wg = jnp.concatenate([_block_diag_heads(lru_wa[l, d]), _block_diag_heads(lru_wx[l, d])], axis=1).astype(BF16)
        bg = jnp.concatenate([lru_ba[l, d], lru_bx[l, d]])[None, :]
        ca = (-LRU_C * jax.nn.softplus(-lru_lambda[l, d]))[None, :]
        lru.append((wg, bg, ca))

    wr = jnp.pad(w_router[l], ((0, 0), (0, V7X_LANES - N_EXPERTS)))
    wrh, wrl = _split_bf16(wr)
    return dict(w1=w1, wq1=wq1, wq2=wq2, wkn=wkn, wkr=wkr, wv=wv, gq=q_norm_g[l][None, :],
                gkv=kv_norm_g[l][None, :], lru=lru, wgl=gl.astype(BF16), wb=w_branch[l].astype(BF16),
                wout=w_out[l].astype(BF16), wrh=wrh, wrl=wrl)


def _sequence_mixers(B, S, proj, lw, cv, lru_conv, tabs, dft, h0, need_branches=True):
    kvin, xb, cq, u, fa, fb, gb = proj
    q, k, v = qkv_proj(kvin, cq, lw['gq'], lw['gkv'], lw['wq1'], lw['wq2'], lw['wkn'], lw['wkr'], lw['wv'],
                       *tabs, B, S)
    conf = fnet = None
    if need_branches:
        conf = conformer(u.reshape(B, S, CONV_WIDTH), *cv)
        fnet = fnet_seq_dft(*dft, fa.reshape(B, S, FNET_WIDTH), fb.reshape(B, S, FNET_WIDTH))
    xb3 = xb.reshape(B, S, LRU_WIDTH)
    hf, s_fwd = lru_scan(xb3, *lru_conv, *lw['lru'][0], h0[0])
    hrec, s_bwd = lru_scan(xb3, *lru_conv, *lw['lru'][1], h0[1], hf=hf, gb=gb.reshape(B, S, LRU_WIDTH))
    return q, k, v, conf, fnet, hrec, (s_fwd, s_bwd)


def _moe(y, h2, aff, gate2, lw_e, ln2, tri, B, N):
    cap = EC_CAPACITY * N // N_EXPERTS
    aff_t = jnp.swapaxes(aff.reshape(B, N, V7X_LANES)[:, :, :N_EXPERTS], 1, 2)
    pos = select_tokens(aff_t, tri, cap)
    ye = experts(pos.reshape(B * N_EXPERTS, 1, N), h2, *lw_e, B, N, cap)
    posc = jnp.pad(jnp.swapaxes(pos, 1, 2), ((0, 0), (0, 0), (0, V7X_LANES - N_EXPERTS)), constant_values=-1.0)
    return combine(y, gate2, posc, aff, ye, *ln2, B, N, cap)


def kernel(x, c, ctx, c_ctx, ada_w, ada_b, w_in, q_norm_g, w_uq, kv_norm_g, w_ukv, cv_w, cv_b, cv_ln_g, cv_ln_b, lru_conv_w, lru_conv_b, lru_wa, lru_ba, lru_wx, lru_bx, lru_lambda, w_branch, w_out, ln1_g, ln1_b, w_router, w_e_gate, w_e_up, w_e_down, ln2_g, ln2_b):
    B, S, _ = x.shape
    SC = ctx.shape[1]
    rot = _rope_rotation()
    cos, sin = _rope_tables(S // GRID_W)
    tabs_l = _position_tables(cos, sin)
    tabs_c = _position_tables(jnp.ones((SC, QK_ROPE), F32), jnp.zeros((SC, QK_ROPE), F32))
    dft_l = _dft_matrices(S)
    dft_c = _dft_matrices(SC)
    wcs = _channel_dft_weights()
    tri = jnp.triu(jnp.ones((V7X_LANES, V7X_LANES), F32), k=1).astype(BF16)
    zero_state = jnp.zeros((B, 1, LRU_WIDTH), F32)

    x_lat = x.reshape(B * S, D_MODEL)
    x_ctx = ctx.reshape(B * SC, D_MODEL)
    tpm_l = S // TOKEN_TILE
    tpm_c = SC // TOKEN_TILE
    for l in range(DEPTH):
        ctx_out = l < DEPTH - 1
        lw = _layer_weights(l, w_in, q_norm_g, w_uq, kv_norm_g, w_ukv, lru_wa, lru_ba, lru_wx, lru_bx,
                            lru_lambda, w_branch, w_out, w_router, rot)
        cv = (cv_w[l], cv_b[l][None, :], cv_ln_g[l][None, :], cv_ln_b[l][None, :])
        lru_conv = (lru_conv_w[l], lru_conv_b[l][None, :])
        ln1 = (ln1_g[l][None, :], ln1_b[l][None, :])
        ln2 = (ln2_g[l][None, :], ln2_b[l][None, :])
        lw_e = (w_e_gate[l].astype(BF16), w_e_up[l].astype(BF16), w_e_down[l].astype(BF16))

        mod = jax.nn.silu(c) @ ada_w[l] + ada_b[l]
        mods_l = [m[:, None, :] for m in jnp.split(mod, 6, axis=-1)]
        mod_c = jax.nn.silu(c_ctx) @ ada_w[l] + ada_b[l]
        mods_c = [m[None, None, :] for m in jnp.split(mod_c, 6)]

        proj_c = in_proj(x_ctx, mods_c[0], mods_c[1], lw['w1'], wcs, tpm_c)
        q_c, k_c, v_c, conf_c, fnet_c, hrec_c, states = _sequence_mixers(
            B, SC, proj_c, lw, cv, lru_conv, tabs_c, dft_c, (zero_state, zero_state), need_branches=ctx_out)
        proj_l = in_proj(x_lat, mods_l[0], mods_l[1], lw['w1'], wcs, tpm_l)
        q_l, k_l, v_l, conf_l, fnet_l, hrec_l, _ = _sequence_mixers(
            B, S, proj_l, lw, cv, lru_conv, tabs_l, dft_l, states)

        attn_l = attention(q_l, [(k_l, v_l), (k_c, v_c)])
        merge_w = (lw['wgl'], lw['wb'], lw['wout'], *ln1, lw['wrh'], lw['wrl'])
        sel = lambda ms: (ms[0], ms[1], ms[2], ms[3], ms[4])
        y_l, h2_l, aff_l = merge(x_lat, sel(mods_l),
                                 [attn_l.reshape(B * S, -1), conf_l.reshape(B * S, -1),
                                  fnet_l.reshape(B * S, -1), hrec_l.reshape(B * S, -1)], *merge_w, tpm_l)
        x_lat = _moe(y_l, h2_l, aff_l, mods_l[5], lw_e, ln2, tri, B, S)
        if ctx_out:
            attn_c = attention(q_c, [(k_c, v_c)])
            y_c, h2_c, aff_c = merge(x_ctx, sel(mods_c),
                                     [attn_c.reshape(B * SC, -1), conf_c.reshape(B * SC, -1),
                                      fnet_c.reshape(B * SC, -1), hrec_c.reshape(B * SC, -1)], *merge_w, tpm_c)
            x_ctx = _moe(y_c, h2_c, aff_c, mods_c[5], lw_e, ln2, tri, B, SC)
    return x_lat.reshape(B, S, D_MODEL)
```

```python
import functools
import math

import jax
import jax.numpy as jnp
from jax import lax
from jax.experimental import pallas as pl
from jax.experimental.pallas import tpu as pltpu

D_MODEL = 1024
DEPTH = 4
GRID_W = 64
MLA_HEADS = 8
Q_LORA = 384
KV_LORA = 256
QK_NOPE = 64
QK_ROPE = 32
V_DIM = 64
ROPE_AXIS_FREQ = QK_ROPE // 4
ROPE_BASE = 10000.0
CONV_WIDTH = 512
CONV_K = 31
FNET_WIDTH = 512
FNET_GROUPS = 4
LRU_WIDTH = 512
LRU_HEADS = 8
LRU_CONV_K = 4
LRU_C = 8.0
N_BRANCH = 4
BRANCH_WIDTH = 512
N_EXPERTS = 16
EXPERT_FF = 1408
EC_CAPACITY = 2
DEEPNORM_ALPHA = (2 * DEPTH) ** 0.25
LN_EPS = 1e-6

V7X_LANES = 128
V7X_SUBLANES = 8
V7X_VMEM_LIMIT_BYTES = 56 * 1024 * 1024
HEAD_SLOT = V7X_LANES
TOKEN_TILE = 256
IN_PROJ_TILE = 512
EXPERT_ROWS = 512
KVIN_WIDTH = KV_LORA + V7X_LANES
IN_PROJ_WIDTH = KVIN_WIDTH + LRU_WIDTH + Q_LORA + 2 * CONV_WIDTH + FNET_WIDTH + LRU_WIDTH

F32 = jnp.float32
BF16 = jnp.bfloat16


def _params(*sem):
    return pltpu.CompilerParams(dimension_semantics=sem, vmem_limit_bytes=V7X_VMEM_LIMIT_BYTES)


def _sigmoid(x):
    return 1.0 / (1.0 + jnp.exp(-x))


def _ln(x):
    mu = jnp.mean(x, axis=-1, keepdims=True)
    xc = x - mu
    var = jnp.mean(xc * xc, axis=-1, keepdims=True)
    return xc * lax.rsqrt(var + LN_EPS)


def _rms(x, g):
    return x * lax.rsqrt(jnp.mean(x * x, axis=-1, keepdims=True) + LN_EPS) * g


def _dot(a, b):
    return jnp.dot(a, b, preferred_element_type=F32)


def _mod_spec(mods, tiles_per_mod):
    if mods.shape[0] == 1:
        return pl.BlockSpec((1, 1, D_MODEL), lambda i: (0, 0, 0))
    return pl.BlockSpec((1, 1, D_MODEL), lambda i: (i // tiles_per_mod, 0, 0))


def _full_spec(arr):
    zeros = (0,) * arr.ndim
    return pl.BlockSpec(arr.shape, lambda *_: zeros)


def _in_proj_body(x_ref, sh_ref, sc_ref, w_ref, wcs_ref,
                  kvin_ref, xb_ref, cq_ref, u_ref, fa_ref, fb_ref, gb_ref):
    h = _ln(x_ref[...]) * (1.0 + sc_ref[0]) + sh_ref[0]
    p = _dot(h.astype(BF16), w_ref[...])
    o = 0
    kvin_ref[...] = p[:, o:o + KVIN_WIDTH]
    o += KVIN_WIDTH
    xb_ref[...] = p[:, o:o + LRU_WIDTH]
    o += LRU_WIDTH
    cq_ref[...] = p[:, o:o + Q_LORA]
    o += Q_LORA
    a = p[:, o:o + CONV_WIDTH]
    g = p[:, o + CONV_WIDTH:o + 2 * CONV_WIDTH]
    u_ref[...] = a * _sigmoid(g)
    o += 2 * CONV_WIDTH
    f = p[:, o:o + FNET_WIDTH].astype(BF16)
    o += FNET_WIDTH
    gb_ref[...] = p[:, o:o + LRU_WIDTH]
    ab = _dot(f, wcs_ref[...])
    fa_ref[...] = ab[:, :FNET_WIDTH].astype(BF16)
    fb_ref[...] = ab[:, FNET_WIDTH:].astype(BF16)


def in_proj(x, shift, scale, w1, wcs, rows_per_mod):
    M = x.shape[0]
    tm = min(IN_PROJ_TILE, rows_per_mod)
    tiles_per_mod = rows_per_mod // tm
    row = lambda i: (i, 0)
    widths = (KVIN_WIDTH, LRU_WIDTH, Q_LORA, CONV_WIDTH, FNET_WIDTH, FNET_WIDTH, LRU_WIDTH)
    dtypes = (F32, F32, F32, F32, BF16, BF16, F32)
    return pl.pallas_call(
        _in_proj_body,
        grid=(M // tm,),
        in_specs=[pl.BlockSpec((tm, D_MODEL), row), _mod_spec(shift, tiles_per_mod),
                  _mod_spec(scale, tiles_per_mod), _full_spec(w1), _full_spec(wcs)],
        out_specs=[pl.BlockSpec((tm, w), row) for w in widths],
        out_shape=[jax.ShapeDtypeStruct((M, w), d) for w, d in zip(widths, dtypes)],
        compiler_params=_params("arbitrary"),
        name="in_proj",
    )(x, shift, scale, w1, wcs)


def _qkv_body(kvin_ref, cq_ref, gq_ref, gkv_ref, wq1_ref, wq2_ref, wkn_ref, wkr_ref, wv_ref,
              qc_ref, qs_ref, kcs_ref, q_ref, k_ref, v_ref):
    nq = _rms(cq_ref[...], gq_ref[...]).astype(BF16)
    q1 = _dot(nq, wq1_ref[...])
    q2 = _dot(nq, wq2_ref[...])
    kvin = kvin_ref[...]
    n = _rms(kvin[:, :KV_LORA], gkv_ref[...]).astype(BF16)
    kr = (kvin[:, KV_LORA:] * kcs_ref[...]).astype(BF16)
    kk = _dot(n, wkn_ref[...]) + _dot(kr, wkr_ref[...])
    v = _dot(n, wv_ref[...])
    qc = qc_ref[...]
    qs = qs_ref[...]
    scale = (QK_NOPE + QK_ROPE) ** -0.5
    for h in range(MLA_HEADS):
        sl = slice(h * HEAD_SLOT, (h + 1) * HEAD_SLOT)
        q_ref[0, h] = ((q1[:, sl] * qc + q2[:, sl] * qs) * scale).astype(BF16)
        k_ref[0, h] = kk[:, sl].astype(BF16)
    for p in range(MLA_HEADS // 2):
        v_ref[0, p] = v[:, p * V7X_LANES:(p + 1) * V7X_LANES].astype(BF16)


def qkv_proj(kvin, cq, gq, gkv, wq1, wq2, wkn, wkr, wv, qc, qs, kcs, B, S):
    tm = TOKEN_TILE
    tps = S // tm
    row = lambda i: (i, 0)
    tab = lambda i: (i % tps, 0)
    seq = lambda i: (i // tps, 0, i % tps, 0)
    return pl.pallas_call(
        _qkv_body,
        grid=(B * tps,),
        in_specs=[pl.BlockSpec((tm, KVIN_WIDTH), row), pl.BlockSpec((tm, Q_LORA), row),
                  _full_spec(gq), _full_spec(gkv), _full_spec(wq1), _full_spec(wq2),
                  _full_spec(wkn), _full_spec(wkr), _full_spec(wv),
                  pl.BlockSpec((tm, HEAD_SLOT), tab), pl.BlockSpec((tm, HEAD_SLOT), tab),
                  pl.BlockSpec((tm, V7X_LANES), tab)],
        out_specs=[pl.BlockSpec((1, MLA_HEADS, tm, HEAD_SLOT), seq),
                   pl.BlockSpec((1, MLA_HEADS, tm, HEAD_SLOT), seq),
                   pl.BlockSpec((1, MLA_HEADS // 2, tm, V7X_LANES), seq)],
        out_shape=[jax.ShapeDtypeStruct((B, MLA_HEADS, S, HEAD_SLOT), BF16),
                   jax.ShapeDtypeStruct((B, MLA_HEADS, S, HEAD_SLOT), BF16),
                   jax.ShapeDtypeStruct((B, MLA_HEADS // 2, S, V7X_LANES), BF16)],
        compiler_params=_params("arbitrary"),
        name="qkv_proj",
    )(kvin, cq, gq, gkv, wq1, wq2, wkn, wkr, wv, qc, qs, kcs)


def _attn_body(*refs, n_sets):
    q_ref = refs[0]
    k_refs = refs[1:1 + 2 * n_sets:2]
    v_refs = refs[2:2 + 2 * n_sets:2]
    o_ref = refs[-1]
    for hp in range(MLA_HEADS // 2):
        outs = []
        for hh in range(2):
            h = 2 * hp + hh
            q = q_ref[0, h]
            s = [lax.dot_general(q, k_ref[0, h], (((1,), (1,)), ((), ())), preferred_element_type=F32)
                 for k_ref in k_refs]
            m = functools.reduce(jnp.maximum, [jnp.max(si, axis=-1, keepdims=True) for si in s])
            p = [jnp.exp(si - m) for si in s]
            l = functools.reduce(jnp.add, [jnp.sum(pi, axis=-1, keepdims=True) for pi in p])
            o = functools.reduce(jnp.add, [_dot(pi.astype(BF16), v_ref[0, hp]) for pi, v_ref in zip(p, v_refs)])
            outs.append(o / l)
        lane = lax.broadcasted_iota(jnp.int32, outs[0].shape, 1)
        o_ref[0, :, hp * V7X_LANES:(hp + 1) * V7X_LANES] = jnp.where(lane < V_DIM, outs[0], outs[1]).astype(BF16)


def attention(q, kv_sets):
    B, H, S, _ = q.shape
    tq = TOKEN_TILE
    in_specs = [pl.BlockSpec((1, H, tq, HEAD_SLOT), lambda b, i: (b, 0, i, 0))]
    args = [q]
    for k, v in kv_sets:
        sk = k.shape[2]
        in_specs.append(pl.BlockSpec((1, H, sk, HEAD_SLOT), lambda b, i: (b, 0, 0, 0)))
        in_specs.append(pl.BlockSpec((1, H // 2, sk, V7X_LANES), lambda b, i: (b, 0, 0, 0)))
        args += [k, v]
    return pl.pallas_call(
        functools.partial(_attn_body, n_sets=len(kv_sets)),
        grid=(B, S // tq),
        in_specs=in_specs,
        out_specs=pl.BlockSpec((1, tq, H * V_DIM), lambda b, i: (b, i, 0)),
        out_shape=jax.ShapeDtypeStruct((B, S, H * V_DIM), BF16),
        compiler_params=_params("arbitrary", "arbitrary"),
        name="attention",
    )(*args)


def _fill_window(win_ref, x_ref, j, n_chunks, tc, halo):
    c = win_ref.shape[1]
    start = pl.multiple_of(j * tc, tc)
    win_ref[halo:halo + tc, :] = x_ref[0, pl.ds(start, tc), :]
    lo = pl.multiple_of(jnp.maximum(j * tc - halo, 0), halo)
    prev = x_ref[0, pl.ds(lo, halo), :]
    win_ref[0:halo, :] = jnp.where(j > 0, prev, jnp.zeros((halo, c), F32))
    hi = pl.multiple_of(jnp.minimum((j + 1) * tc, (n_chunks - 1) * tc + tc - halo), halo)
    nxt = x_ref[0, pl.ds(hi, halo), :]
    win_ref[halo + tc:2 * halo + tc, :] = jnp.where(j < n_chunks - 1, nxt, jnp.zeros((halo, c), F32))


def _conv_taps(win_ref, w_ref, b_ref, out_ref, tc, first_row, n_taps, row_block=64):
    c = win_ref.shape[1]
    for r0 in range(0, tc, row_block):
        for l0 in range(0, c, V7X_LANES):
            ls = slice(l0, l0 + V7X_LANES)
            acc = jnp.broadcast_to(b_ref[:, ls], (row_block, V7X_LANES))
            for k in range(n_taps):
                r = first_row + r0 + k
                acc = acc + win_ref[r:r + row_block, ls] * w_ref[k:k + 1, ls]
            out_ref[r0:r0 + row_block, ls] = acc


CONF_HALO = 16
CONF_ROW_BLOCK = 64


def _conf_body(u_ref, w_ref, b_ref, g_ref, beta_ref, o_ref, win_ref, sh_ref, acc_ref, *, n_chunks, tc):
    j = pl.program_id(1)
    _fill_window(win_ref, u_ref, j, n_chunks, tc, CONF_HALO)
    span = sh_ref.shape[1]
    for s in range(V7X_SUBLANES):
        sh_ref[s] = win_ref[s:s + span, :]
    first_row = CONF_HALO - (CONV_K - 1) // 2
    c = win_ref.shape[1]
    for r0 in range(0, tc, CONF_ROW_BLOCK):
        for l0 in range(0, c, V7X_LANES):
            ls = slice(l0, l0 + V7X_LANES)
            acc = jnp.broadcast_to(b_ref[:, ls], (CONF_ROW_BLOCK, V7X_LANES))
            for k in range(CONV_K):
                phase = (first_row + k) % V7X_SUBLANES
                base = r0 + (first_row + k) // V7X_SUBLANES * V7X_SUBLANES
                acc = acc + sh_ref[phase, base:base + CONF_ROW_BLOCK, ls] * w_ref[k:k + 1, ls]
            acc_ref[r0:r0 + CONF_ROW_BLOCK, ls] = acc
    y = _ln(acc_ref[...]) * g_ref[...] + beta_ref[...]
    o_ref[0] = (y * _sigmoid(y)).astype(BF16)


def conformer(u, w, b, g, beta):
    B, S, C = u.shape
    tc = TOKEN_TILE
    n_chunks = S // tc
    span = tc + 2 * CONF_HALO - V7X_SUBLANES
    return pl.pallas_call(
        functools.partial(_conf_body, n_chunks=n_chunks, tc=tc),
        grid=(B, n_chunks),
        in_specs=[pl.BlockSpec((1, S, C), lambda bb, j: (bb, 0, 0)),
                  _full_spec(w), _full_spec(b), _full_spec(g), _full_spec(beta)],
        out_specs=pl.BlockSpec((1, tc, C), lambda bb, j: (bb, j, 0)),
        out_shape=jax.ShapeDtypeStruct((B, S, C), BF16),
        scratch_shapes=[pltpu.VMEM((tc + 2 * CONF_HALO, C), F32), pltpu.VMEM((V7X_SUBLANES, span, C), F32),
                        pltpu.VMEM((tc, C), F32)],
        compiler_params=_params("arbitrary", "arbitrary"),
        name="conformer",
    )(u, w, b, g, beta)


def _fnet_body(c_ref, s_ref, a_ref, b_ref, o_ref, *, norm):
    y = _dot(c_ref[...], a_ref[0]) + _dot(s_ref[...], b_ref[0])
    o_ref[0] = (y * norm).astype(BF16)


def fnet_seq_dft(cmat, smat, fa, fb):
    B, S, W = fa.shape
    ts = min(S, 512)
    norm = 1.0 / math.sqrt(S * (FNET_WIDTH // FNET_GROUPS))
    return pl.pallas_call(
        functools.partial(_fnet_body, norm=norm),
        grid=(S // ts, B),
        in_specs=[pl.BlockSpec((ts, S), lambda i, b: (i, 0)), pl.BlockSpec((ts, S), lambda i, b: (i, 0)),
                  pl.BlockSpec((1, S, W), lambda i, b: (b, 0, 0)), pl.BlockSpec((1, S, W), lambda i, b: (b, 0, 0))],
        out_specs=pl.BlockSpec((1, ts, W), lambda i, b: (b, i, 0)),
        out_shape=jax.ShapeDtypeStruct((B, S, W), BF16),
        compiler_params=_params("arbitrary", "arbitrary"),
        name="fnet_seq_dft",
    )(cmat, smat, fa, fb)


LRU_HALO = 8
LRU_GROUP = V7X_SUBLANES
LRU_UNROLL = 8


def _lru_body(*refs, n_chunks, tc, reverse):
    if reverse:
        (x_ref, cw_ref, cb_ref, wg_ref, bg_ref, ca_ref, h0_ref, hf_ref, gb_ref,
         o_ref, s_ref, win_ref, xc_ref, a_ref, u_ref, carry_ref) = refs
    else:
        (x_ref, cw_ref, cb_ref, wg_ref, bg_ref, ca_ref, h0_ref,
         o_ref, s_ref, win_ref, xc_ref, a_ref, u_ref, carry_ref) = refs
    step = pl.program_id(1)
    j = (n_chunks - 1 - step) if reverse else step

    @pl.when(step == 0)
    def _():
        carry_ref[...] = jnp.broadcast_to(h0_ref[0], carry_ref.shape)

    _fill_window(win_ref, x_ref, j, n_chunks, tc, LRU_HALO)
    _conv_taps(win_ref, cw_ref, cb_ref, xc_ref, tc, LRU_HALO - LRU_CONV_K // 2, LRU_CONV_K)
    xc = xc_ref[...]
    gates = _dot(xc.astype(BF16), wg_ref[...]) + bg_ref[...]
    r = _sigmoid(gates[:, :LRU_WIDTH])
    i = _sigmoid(gates[:, LRU_WIDTH:])
    log_a = ca_ref[...] * r
    a = jnp.exp(log_a)
    a_ref[...] = a
    u_ref[...] = jnp.sqrt(-jnp.tanh(log_a) * (a * a + 1.0)) * (i * xc)

    n_groups = tc // LRU_GROUP
    row = lax.broadcasted_iota(jnp.int32, (LRU_GROUP, LRU_WIDTH), 0)

    def group(gi, carry):
        g = (n_groups - 1 - gi) if reverse else gi
        rows = pl.ds(pl.multiple_of(g * LRU_GROUP, LRU_GROUP), LRU_GROUP)
        a = a_ref[rows, :]
        u = u_ref[rows, :]
        for sh in (1, 2, 4):
            if reverse:
                a_sh = pltpu.roll(a, LRU_GROUP - sh, 0)
                u_sh = pltpu.roll(u, LRU_GROUP - sh, 0)
                valid = row < LRU_GROUP - sh
            else:
                a_sh = pltpu.roll(a, sh, 0)
                u_sh = pltpu.roll(u, sh, 0)
                valid = row >= sh
            a_sh = jnp.where(valid, a_sh, 1.0)
            u_sh = jnp.where(valid, u_sh, 0.0)
            u = a * u_sh + u
            a = a * a_sh
        h = u + a * carry
        u_ref[rows, :] = h
        last = h[0:1] if reverse else h[LRU_GROUP - 1:LRU_GROUP]
        return jnp.broadcast_to(last, h.shape)

    carry = lax.fori_loop(0, n_groups, group, carry_ref[...], unroll=LRU_UNROLL)
    carry_ref[...] = carry
    h = u_ref[...]
    if reverse:
        gb = gb_ref[0]
        gelu = 0.5 * gb * (1.0 + jnp.tanh(math.sqrt(2.0 / math.pi) * (gb + 0.044715 * (gb * gb * gb))))
        o_ref[0] = ((hf_ref[0] + h) * gelu).astype(BF16)
    else:
        o_ref[0] = h

    @pl.when(step == n_chunks - 1)
    def _():
        s_ref[0] = carry[0:1]


def lru_scan(xb, cw, cb, wg, bg, ca, h0, hf=None, gb=None):
    B, S, W = xb.shape
    reverse = hf is not None
    tc = TOKEN_TILE
    n_chunks = S // tc
    chunk = (lambda bb, s: (bb, n_chunks - 1 - s, 0)) if reverse else (lambda bb, s: (bb, s, 0))
    per_b = lambda bb, s: (bb, 0, 0)
    in_specs = [pl.BlockSpec((1, S, W), per_b), _full_spec(cw), _full_spec(cb), _full_spec(wg),
                _full_spec(bg), _full_spec(ca), pl.BlockSpec((1, 1, W), per_b)]
    args = [xb, cw, cb, wg, bg, ca, h0]
    if reverse:
        in_specs += [pl.BlockSpec((1, tc, W), chunk), pl.BlockSpec((1, tc, W), chunk)]
        args += [hf, gb]
    return pl.pallas_call(
        functools.partial(_lru_body, n_chunks=n_chunks, tc=tc, reverse=reverse),
        grid=(B, n_chunks),
        in_specs=in_specs,
        out_specs=[pl.BlockSpec((1, tc, W), chunk), pl.BlockSpec((1, 1, W), per_b)],
        out_shape=[jax.ShapeDtypeStruct((B, S, W), BF16 if reverse else F32),
                   jax.ShapeDtypeStruct((B, 1, W), F32)],
        scratch_shapes=[pltpu.VMEM((tc + 2 * LRU_HALO, W), F32), pltpu.VMEM((tc, W), F32),
                        pltpu.VMEM((tc, W), F32), pltpu.VMEM((tc, W), F32),
                        pltpu.VMEM((LRU_GROUP, W), F32)],
        compiler_params=_params("arbitrary", "arbitrary"),
        name="lru_bwd" if reverse else "lru_fwd",
    )(*args)


def _split_bf16(x):
    hi = x.astype(BF16)
    return hi, (x - hi.astype(F32)).astype(BF16)


def _merge_body(x_ref, sh1_ref, sc1_ref, g1_ref, sh2_ref, sc2_ref, attn_ref, conf_ref, fnet_ref, lru_ref,
                wgl_ref, wb_ref, wout_ref, lng_ref, lnb_ref, wrh_ref, wrl_ref,
                y_ref, h2_ref, aff_ref):
    x = x_ref[...]
    h = (_ln(x) * (1.0 + sc1_ref[0]) + sh1_ref[0]).astype(BF16)
    acc = None
    for k, br_ref in enumerate((attn_ref, conf_ref, fnet_ref, lru_ref)):
        gl = _dot(h, wgl_ref[:, k * D_MODEL:(k + 1) * D_MODEL])
        term = _sigmoid(gl) * _dot(br_ref[...], wb_ref[k])
        acc = term if acc is None else acc + term
    out = _dot(acc.astype(BF16), wout_ref[...])
    y = _ln(DEEPNORM_ALPHA * x + g1_ref[0] * out) * lng_ref[...] + lnb_ref[...]
    y_ref[...] = y
    h2 = _ln(y) * (1.0 + sc2_ref[0]) + sh2_ref[0]
    h2_ref[...] = h2.astype(BF16)
    hh, hl = _split_bf16(h2)
    logits = _dot(hh, wrh_ref[...]) + _dot(hh, wrl_ref[...]) + _dot(hl, wrh_ref[...])
    lane = lax.broadcasted_iota(jnp.int32, logits.shape, 1)
    valid = lane < N_EXPERTS
    logits = jnp.where(valid, logits, -jnp.inf)
    e = jnp.exp(logits - jnp.max(logits, axis=-1, keepdims=True))
    aff_ref[...] = e / jnp.sum(e, axis=-1, keepdims=True)


def merge(x, mods, branches, wgl, wb, wout, lng, lnb, wrh, wrl, tiles_per_mod):
    M = x.shape[0]
    tm = TOKEN_TILE
    row = lambda i: (i, 0)
    sh1, sc1, g1, sh2, sc2 = mods
    in_specs = ([pl.BlockSpec((tm, D_MODEL), row)] + [_mod_spec(m, tiles_per_mod) for m in mods]
                + [pl.BlockSpec((tm, BRANCH_WIDTH), row)] * N_BRANCH
                + [_full_spec(a) for a in (wgl, wb, wout, lng, lnb, wrh, wrl)])
    return pl.pallas_call(
        _merge_body,
        grid=(M // tm,),
        in_specs=in_specs,
        out_specs=[pl.BlockSpec((tm, D_MODEL), row), pl.BlockSpec((tm, D_MODEL), row),
                   pl.BlockSpec((tm, V7X_LANES), row)],
        out_shape=[jax.ShapeDtypeStruct((M, D_MODEL), F32), jax.ShapeDtypeStruct((M, D_MODEL), BF16),
                   jax.ShapeDtypeStruct((M, V7X_LANES), F32)],
        compiler_params=_params("arbitrary"),
        name="merge",
    )(x, sh1, sc1, g1, sh2, sc2, *branches, wgl, wb, wout, lng, lnb, wrh, wrl)


def _select_body(aff_ref, tri_ref, pos_ref, *, cap, n_tok):
    bits = pltpu.bitcast(aff_ref[0], jnp.int32)

    def bisect(it, lo):
        cand = lo | jnp.left_shift(jnp.int32(1), 30 - it)
        cnt = jnp.sum(jnp.where(bits >= cand, 1.0, 0.0), axis=1, keepdims=True)
        return jnp.where(cnt >= cap, cand, lo)

    thr = lax.fori_loop(0, 31, bisect, jnp.zeros((N_EXPERTS, 1), jnp.int32))
    need = cap - jnp.sum(jnp.where(bits > thr, 1.0, 0.0), axis=1, keepdims=True)
    tri = tri_ref[...]
    ties_before = jnp.zeros((N_EXPERTS, 1), F32)
    sel_before = jnp.zeros((N_EXPERTS, 1), F32)
    for c0 in range(0, n_tok, V7X_LANES):
        sl = slice(c0, c0 + V7X_LANES)
        bits_c = bits[:, sl]
        gt_c = jnp.where(bits_c > thr, 1.0, 0.0)
        eq_c = jnp.where(bits_c == thr, 1.0, 0.0)
        tie_rank = _dot(eq_c.astype(BF16), tri) + ties_before
        ties_before = ties_before + jnp.sum(eq_c, axis=1, keepdims=True)
        sel_c = gt_c + eq_c * jnp.where(tie_rank < need, 1.0, 0.0)
        slot = _dot(sel_c.astype(BF16), tri) + sel_before
        sel_before = sel_before + jnp.sum(sel_c, axis=1, keepdims=True)
        pos_ref[0, :, sl] = jnp.where(sel_c > 0.0, slot, -1.0)


def select_tokens(aff_t, tri, cap):
    B, E, N = aff_t.shape
    return pl.pallas_call(
        functools.partial(_select_body, cap=cap, n_tok=N),
        grid=(B,),
        in_specs=[pl.BlockSpec((1, E, N), lambda b: (b, 0, 0)), _full_spec(tri)],
        out_specs=pl.BlockSpec((1, E, N), lambda b: (b, 0, 0)),
        out_shape=jax.ShapeDtypeStruct((B, E, N), F32),
        compiler_params=_params("arbitrary"),
        name="select_tokens",
    )(aff_t, tri)


def _experts_body(pos_ref, h_ref, wg_ref, wu_ref, wd_ref, y_ref, *, rows):
    c = pl.program_id(2)
    slot = lax.broadcasted_iota(jnp.int32, (rows, 1), 0) + c * rows
    pos = pos_ref[0].astype(jnp.int32)
    onehot = jnp.where(slot == pos, 1.0, 0.0).astype(BF16)
    xe = _dot(onehot, h_ref[...]).astype(BF16)
    g = _dot(xe, wg_ref[0])
    u = _dot(xe, wu_ref[0])
    he = (g * _sigmoid(g) * u).astype(BF16)
    y_ref[0, 0] = _dot(he, wd_ref[0]).astype(BF16)


def experts(pos_rows, h2, wg, wu, wd, B, N, cap):
    rows = min(cap, EXPERT_ROWS)
    n_c = cap // rows
    E = N_EXPERTS
    return pl.pallas_call(
        functools.partial(_experts_body, rows=rows),
        grid=(E, B, n_c),
        in_specs=[pl.BlockSpec((1, 1, N), lambda e, b, c: (b * E + e, 0, 0)),
                  pl.BlockSpec((N, D_MODEL), lambda e, b, c: (b, 0)),
                  pl.BlockSpec((1, D_MODEL, EXPERT_FF), lambda e, b, c: (e, 0, 0)),
                  pl.BlockSpec((1, D_MODEL, EXPERT_FF), lambda e, b, c: (e, 0, 0)),
                  pl.BlockSpec((1, EXPERT_FF, D_MODEL), lambda e, b, c: (e, 0, 0))],
        out_specs=pl.BlockSpec((1, 1, rows, D_MODEL), lambda e, b, c: (b, e, c, 0)),
        out_shape=jax.ShapeDtypeStruct((B, E, cap, D_MODEL), BF16),
        compiler_params=_params("arbitrary", "arbitrary", "arbitrary"),
        name="experts",
    )(pos_rows, h2, wg, wu, wd)


def _combine_body(x_ref, g2_ref, posc_ref, aff_ref, ye_ref, lng_ref, lnb_ref, o_ref, *, cap):
    posc = posc_ref[0]
    aff = aff_ref[...]
    slot = lax.broadcasted_iota(jnp.int32, (1, cap), 1).astype(F32)
    acc = None
    for e in range(N_EXPERTS):
        pcol = posc[:, e:e + 1]
        onehot = jnp.where(pcol == slot, 1.0, 0.0).astype(BF16)
        w = jnp.where(pcol >= 0.0, aff[:, e:e + 1], 0.0)
        term = w * _dot(onehot, ye_ref[0, e])
        acc = term if acc is None else acc + term
    y = _ln(DEEPNORM_ALPHA * x_ref[...] + g2_ref[0] * acc)
    o_ref[...] = y * lng_ref[...] + lnb_ref[...]


def combine(x, gate2, posc, aff, ye, lng, lnb, B, N, cap):
    tm = min(N, 512)
    tpb = N // tm
    row = lambda i: (i, 0)
    gate_spec = (pl.BlockSpec((1, 1, D_MODEL), lambda i: (0, 0, 0)) if gate2.shape[0] == 1
                 else pl.BlockSpec((1, 1, D_MODEL), lambda i: (i // tpb, 0, 0)))
    return pl.pallas_call(
        functools.partial(_combine_body, cap=cap),
        grid=(B * tpb,),
        in_specs=[pl.BlockSpec((tm, D_MODEL), row), gate_spec,
                  pl.BlockSpec((1, tm, V7X_LANES), lambda i: (i // tpb, i % tpb, 0)),
                  pl.BlockSpec((tm, V7X_LANES), row),
                  pl.BlockSpec((1, N_EXPERTS, cap, D_MODEL), lambda i: (i // tpb, 0, 0, 0)),
                  _full_spec(lng), _full_spec(lnb)],
        out_specs=pl.BlockSpec((tm, D_MODEL), row),
        out_shape=jax.ShapeDtypeStruct((B * N, D_MODEL), F32),
        compiler_params=_params("arbitrary"),
        name="combine",
    )(x, gate2, posc, aff, ye, lng, lnb)


def _rope_rotation():
    r = [[0.0] * QK_ROPE for _ in range(QK_ROPE)]
    half = QK_ROPE // 4
    for base in (0, QK_ROPE // 2):
        for j in range(half):
            r[base + j + half][base + j] = -1.0
            r[base + j][base + j + half] = 1.0
    return jnp.array(r, F32)


def _rope_tables(rows):
    row = jnp.repeat(jnp.arange(rows), GRID_W).astype(F32)
    col = jnp.tile(jnp.arange(GRID_W), rows).astype(F32)
    inv = ROPE_BASE ** (-jnp.arange(ROPE_AXIS_FREQ, dtype=F32) / ROPE_AXIS_FREQ)
    ang_r = row[:, None] * inv
    ang_c = col[:, None] * inv
    cos = jnp.concatenate([jnp.cos(ang_r)] * 2 + [jnp.cos(ang_c)] * 2, axis=-1)
    sin = jnp.concatenate([jnp.sin(ang_r)] * 2 + [jnp.sin(ang_c)] * 2, axis=-1)
    return cos, sin


def _position_tables(cos, sin):
    S = cos.shape[0]
    ones = jnp.ones((S, QK_NOPE), F32)
    pad = jnp.zeros((S, HEAD_SLOT - QK_NOPE - QK_ROPE), F32)
    qc = jnp.concatenate([ones, cos, pad], axis=-1)
    qs = jnp.concatenate([jnp.zeros((S, QK_NOPE), F32), sin, pad], axis=-1)
    kcs = jnp.concatenate([cos, sin, jnp.zeros((S, V7X_LANES - 2 * QK_ROPE), F32)], axis=-1)
    return qc, qs, kcs


def _dft_matrices(S):
    k = jnp.arange(S, dtype=jnp.int32)
    ang = ((k[:, None] * k[None, :]) % S).astype(F32) * (2.0 * math.pi / S)
    return jnp.cos(ang).astype(BF16), (-jnp.sin(ang)).astype(BF16)


def _channel_dft_weights():
    n = FNET_WIDTH // FNET_GROUPS
    k = jnp.arange(n, dtype=jnp.int32)
    ang = ((k[:, None] * k[None, :]) % n).astype(F32) * (2.0 * math.pi / n)
    eye = jnp.eye(FNET_GROUPS, dtype=F32)
    return jnp.concatenate([jnp.kron(eye, jnp.cos(ang)), jnp.kron(eye, jnp.sin(ang))], axis=1).astype(BF16)


def _block_diag_heads(w):
    H, d, _ = w.shape
    out = jnp.zeros((H, d, H, d), w.dtype)
    out = out.at[jnp.arange(H), :, jnp.arange(H), :].set(w)
    return out.reshape(H * d, H * d)


def _layer_weights(l, w_in, q_norm_g, w_uq, kv_norm_g, w_ukv, lru_wa, lru_ba, lru_wx, lru_bx, lru_lambda,
                   w_branch, w_out, w_router, rot):
    wi = w_in[l]
    o = 0
    ckv = wi[:, o:o + KV_LORA]
    o += KV_LORA
    kr = wi[:, o:o + QK_ROPE]
    o += QK_ROPE
    xb = wi[:, o:o + LRU_WIDTH]
    o += LRU_WIDTH
    cq = wi[:, o:o + Q_LORA]
    o += Q_LORA
    glu = wi[:, o:o + 2 * CONV_WIDTH]
    o += 2 * CONV_WIDTH
    f = wi[:, o:o + FNET_WIDTH]
    o += FNET_WIDTH
    gb = wi[:, o:o + LRU_WIDTH]
    o += LRU_WIDTH
    gl = wi[:, o:]
    kpad = jnp.zeros((D_MODEL, V7X_LANES - 2 * QK_ROPE), F32)
    w1 = jnp.concatenate([ckv, kr, kr @ rot, kpad, xb, cq, glu, f, gb], axis=1).astype(BF16)

    uq = w_uq[l]
    zq = jnp.zeros((Q_LORA, MLA_HEADS, HEAD_SLOT - QK_NOPE - QK_ROPE), F32)
    wq1 = jnp.concatenate([uq, zq], axis=-1).reshape(Q_LORA, MLA_HEADS * HEAD_SLOT).astype(BF16)
    uq_rot = jnp.einsum('rhd,de->rhe', uq[..., QK_NOPE:], rot)
    wq2 = jnp.concatenate([jnp.zeros((Q_LORA, MLA_HEADS, QK_NOPE), F32), uq_rot, zq], axis=-1)
    wq2 = wq2.reshape(Q_LORA, MLA_HEADS * HEAD_SLOT).astype(BF16)
    ukv = w_ukv[l]
    zk = jnp.zeros((KV_LORA, MLA_HEADS, HEAD_SLOT - QK_NOPE), F32)
    wkn = jnp.concatenate([ukv[..., :QK_NOPE], zk], axis=-1).reshape(KV_LORA, MLA_HEADS * HEAD_SLOT).astype(BF16)
    wv = ukv[..., QK_NOPE:].reshape(KV_LORA, MLA_HEADS * V_DIM).astype(BF16)
    eye = jnp.eye(QK_ROPE, dtype=F32)
    slot = jnp.concatenate([jnp.zeros((QK_ROPE, QK_NOPE), F32), eye,
                            jnp.zeros((QK_ROPE, HEAD_SLOT - QK_NOPE - QK_ROPE), F32)], axis=1)
    slot = jnp.tile(slot, (1, MLA_HEADS))
    wkr = jnp.concatenate([slot, slot, jnp.zeros((V7X_LANES - 2 * QK_ROPE, MLA_HEADS * HEAD_SLOT), F32)],
                          axis=0).astype(BF16)

    lru = []
    for d in range(2):
        wg = jnp.concatenate([_block_diag_heads(lru_wa[l, d]), _block_diag_heads(lru_wx[l, d])], axis=1).astype(BF16)
        bg = jnp.concatenate([lru_ba[l, d], lru_bx[l, d]])[None, :]
        ca = (-LRU_C * jax.nn.softplus(-lru_lambda[l, d]))[None, :]
        lru.append((wg, bg, ca))

    wr = jnp.pad(w_router[l], ((0, 0), (0, V7X_LANES - N_EXPERTS)))
    wrh, wrl = _split_bf16(wr)
    return dict(w1=w1, wq1=wq1, wq2=wq2, wkn=wkn, wkr=wkr, wv=wv, gq=q_norm_g[l][None, :],
                gkv=kv_norm_g[l][None, :], lru=lru, wgl=gl.astype(BF16), wb=w_branch[l].astype(BF16),
                wout=w_out[l].astype(BF16), wrh=wrh, wrl=wrl)


def _sequence_mixers(B, S, proj, lw, cv, lru_conv, tabs, dft, h0, need_branches=True):
    kvin, xb, cq, u, fa, fb, gb = proj
    q, k, v = qkv_proj(kvin, cq, lw['gq'], lw['gkv'], lw['wq1'], lw['wq2'], lw['wkn'], lw['wkr'], lw['wv'],
                       *tabs, B, S)
    conf = fnet = None
    if need_branches:
        conf = conformer(u.reshape(B, S, CONV_WIDTH), *cv)
        fnet = fnet_seq_dft(*dft, fa.reshape(B, S, FNET_WIDTH), fb.reshape(B, S, FNET_WIDTH))
    xb3 = xb.reshape(B, S, LRU_WIDTH)
    hf, s_fwd = lru_scan(xb3, *lru_conv, *lw['lru'][0], h0[0])
    hrec, s_bwd = lru_scan(xb3, *lru_conv, *lw['lru'][1], h0[1], hf=hf, gb=gb.reshape(B, S, LRU_WIDTH))
    return q, k, v, conf, fnet, hrec, (s_fwd, s_bwd)


def _moe(y, h2, aff, gate2, lw_e, ln2, tri, B, N):
    cap = EC_CAPACITY * N // N_EXPERTS
    aff_t = jnp.swapaxes(aff.reshape(B, N, V7X_LANES)[:, :, :N_EXPERTS], 1, 2)
    pos = select_tokens(aff_t, tri, cap)
    ye = experts(pos.reshape(B * N_EXPERTS, 1, N), h2, *lw_e, B, N, cap)
    posc = jnp.pad(jnp.swapaxes(pos, 1, 2), ((0, 0), (0, 0), (0, V7X_LANES - N_EXPERTS)), constant_values=-1.0)
    return combine(y, gate2, posc, aff, ye, *ln2, B, N, cap)


def kernel(x, c, ctx, c_ctx, ada_w, ada_b, w_in, q_norm_g, w_uq, kv_norm_g, w_ukv, cv_w, cv_b, cv_ln_g, cv_ln_b, lru_conv_w, lru_conv_b, lru_wa, lru_ba, lru_wx, lru_bx, lru_lambda, w_branch, w_out, ln1_g, ln1_b, w_router, w_e_gate, w_e_up, w_e_down, ln2_g, ln2_b):
    B, S, _ = x.shape
    SC = ctx.shape[1]
    rot = _rope_rotation()
    cos, sin = _rope_tables(S // GRID_W)
    tabs_l = _position_tables(cos, sin)
    tabs_c = _position_tables(jnp.ones((SC, QK_ROPE), F32), jnp.zeros((SC, QK_ROPE), F32))
    dft_l = _dft_matrices(S)
    dft_c = _dft_matrices(SC)
    wcs = _channel_dft_weights()
    tri = jnp.triu(jnp.ones((V7X_LANES, V7X_LANES), F32), k=1).astype(BF16)
    zero_state = jnp.zeros((B, 1, LRU_WIDTH), F32)

    x_lat = x.reshape(B * S, D_MODEL)
    x_ctx = ctx.reshape(B * SC, D_MODEL)
    tpm_l = S // TOKEN_TILE
    tpm_c = SC // TOKEN_TILE
    for l in range(DEPTH):
        ctx_out = l < DEPTH - 1
        lw = _layer_weights(l, w_in, q_norm_g, w_uq, kv_norm_g, w_ukv, lru_wa, lru_ba, lru_wx, lru_bx,
                            lru_lambda, w_branch, w_out, w_router, rot)
        cv = (cv_w[l], cv_b[l][None, :], cv_ln_g[l][None, :], cv_ln_b[l][None, :])
        lru_conv = (lru_conv_w[l], lru_conv_b[l][None, :])
        ln1 = (ln1_g[l][None, :], ln1_b[l][None, :])
        ln2 = (ln2_g[l][None, :], ln2_b[l][None, :])
        lw_e = (w_e_gate[l].astype(BF16), w_e_up[l].astype(BF16), w_e_down[l].astype(BF16))

        mod = jax.nn.silu(c) @ ada_w[l] + ada_b[l]
        mods_l = [m[:, None, :] for m in jnp.split(mod, 6, axis=-1)]
        mod_c = jax.nn.silu(c_ctx) @ ada_w[l] + ada_b[l]
        mods_c = [m[None, None, :] for m in jnp.split(mod_c, 6)]

        proj_c = in_proj(x_ctx, mods_c[0], mods_c[1], lw['w1'], wcs, SC)
        q_c, k_c, v_c, conf_c, fnet_c, hrec_c, states = _sequence_mixers(
            B, SC, proj_c, lw, cv, lru_conv, tabs_c, dft_c, (zero_state, zero_state), need_branches=ctx_out)
        proj_l = in_proj(x_lat, mods_l[0], mods_l[1], lw['w1'], wcs, S)
        q_l, k_l, v_l, conf_l, fnet_l, hrec_l, _ = _sequence_mixers(
            B, S, proj_l, lw, cv, lru_conv, tabs_l, dft_l, states)

        attn_l = attention(q_l, [(k_l, v_l), (k_c, v_c)])
        merge_w = (lw['wgl'], lw['wb'], lw['wout'], *ln1, lw['wrh'], lw['wrl'])
        sel = lambda ms: (ms[0], ms[1], ms[2], ms[3], ms[4])
        y_l, h2_l, aff_l = merge(x_lat, sel(mods_l),
                                 [attn_l.reshape(B * S, -1), conf_l.reshape(B * S, -1),
                                  fnet_l.reshape(B * S, -1), hrec_l.reshape(B * S, -1)], *merge_w, tpm_l)
        x_lat = _moe(y_l, h2_l, aff_l, mods_l[5], lw_e, ln2, tri, B, S)
        if ctx_out:
            attn_c = attention(q_c, [(k_c, v_c)])
            y_c, h2_c, aff_c = merge(x_ctx, sel(mods_c),
                                     [attn_c.reshape(B * SC, -1), conf_c.reshape(B * SC, -1),
                                      fnet_c.reshape(B * SC, -1), hrec_c.reshape(B * SC, -1)], *merge_w, tpm_c)
            x_ctx = _moe(y_c, h2_c, aff_c, mods_c[5], lw_e, ln2, tri, B, SC)
    return x_lat.reshape(B, S, D_MODEL)
```

```python
import functools
import math

import jax
import jax.numpy as jnp
from jax import lax
from jax.experimental import pallas as pl
from jax.experimental.pallas import tpu as pltpu

D_MODEL = 1024
DEPTH = 4
GRID_W = 64
MLA_HEADS = 8
Q_LORA = 384
KV_LORA = 256
QK_NOPE = 64
QK_ROPE = 32
V_DIM = 64
ROPE_AXIS_FREQ = QK_ROPE // 4
ROPE_BASE = 10000.0
CONV_WIDTH = 512
CONV_K = 31
FNET_WIDTH = 512
FNET_GROUPS = 4
LRU_WIDTH = 512
LRU_HEADS = 8
LRU_CONV_K = 4
LRU_C = 8.0
N_BRANCH = 4
BRANCH_WIDTH = 512
N_EXPERTS = 16
EXPERT_FF = 1408
EC_CAPACITY = 2
DEEPNORM_ALPHA = (2 * DEPTH) ** 0.25
LN_EPS = 1e-6

V7X_LANES = 128
V7X_SUBLANES = 8
V7X_VMEM_LIMIT_BYTES = 56 * 1024 * 1024
HEAD_SLOT = V7X_LANES
TOKEN_TILE = 256
IN_PROJ_TILE = 512
EXPERT_SLOT_GROUP = 128
EXPERT_TOKEN_BLOCK = 256
KVIN_WIDTH = KV_LORA + V7X_LANES
IN_PROJ_WIDTH = KVIN_WIDTH + LRU_WIDTH + Q_LORA + 2 * CONV_WIDTH + FNET_WIDTH + LRU_WIDTH

F32 = jnp.float32
BF16 = jnp.bfloat16


def _params(*sem):
    return pltpu.CompilerParams(dimension_semantics=sem, vmem_limit_bytes=V7X_VMEM_LIMIT_BYTES)


def _sigmoid(x):
    return 1.0 / (1.0 + jnp.exp(-x))


def _ln(x):
    mu = jnp.mean(x, axis=-1, keepdims=True)
    xc = x - mu
    var = jnp.mean(xc * xc, axis=-1, keepdims=True)
    return xc * lax.rsqrt(var + LN_EPS)


def _rms(x, g):
    return x * lax.rsqrt(jnp.mean(x * x, axis=-1, keepdims=True) + LN_EPS) * g


def _dot(a, b):
    return jnp.dot(a, b, preferred_element_type=F32)


def _mod_spec(mods, tiles_per_mod):
    if mods.shape[0] == 1:
        return pl.BlockSpec((1, 1, D_MODEL), lambda i: (0, 0, 0))
    return pl.BlockSpec((1, 1, D_MODEL), lambda i: (i // tiles_per_mod, 0, 0))


def _full_spec(arr):
    zeros = (0,) * arr.ndim
    return pl.BlockSpec(arr.shape, lambda *_: zeros)


def _in_proj_body(x_ref, sh_ref, sc_ref, w_ref, wcs_ref,
                  kvin_ref, xb_ref, cq_ref, u_ref, fa_ref, fb_ref, gb_ref):
    h = _ln(x_ref[...]) * (1.0 + sc_ref[0]) + sh_ref[0]
    p = _dot(h.astype(BF16), w_ref[...])
    o = 0
    kvin_ref[...] = p[:, o:o + KVIN_WIDTH]
    o += KVIN_WIDTH
    xb_ref[...] = p[:, o:o + LRU_WIDTH]
    o += LRU_WIDTH
    cq_ref[...] = p[:, o:o + Q_LORA]
    o += Q_LORA
    a = p[:, o:o + CONV_WIDTH]
    g = p[:, o + CONV_WIDTH:o + 2 * CONV_WIDTH]
    u_ref[...] = a * _sigmoid(g)
    o += 2 * CONV_WIDTH
    f = p[:, o:o + FNET_WIDTH].astype(BF16)
    o += FNET_WIDTH
    gb_ref[...] = p[:, o:o + LRU_WIDTH]
    ab = _dot(f, wcs_ref[...])
    fa_ref[...] = ab[:, :FNET_WIDTH].astype(BF16)
    fb_ref[...] = ab[:, FNET_WIDTH:].astype(BF16)


def in_proj(x, shift, scale, w1, wcs, rows_per_mod):
    M = x.shape[0]
    tm = min(IN_PROJ_TILE, rows_per_mod)
    tiles_per_mod = rows_per_mod // tm
    row = lambda i: (i, 0)
    widths = (KVIN_WIDTH, LRU_WIDTH, Q_LORA, CONV_WIDTH, FNET_WIDTH, FNET_WIDTH, LRU_WIDTH)
    dtypes = (F32, F32, F32, F32, BF16, BF16, F32)
    return pl.pallas_call(
        _in_proj_body,
        grid=(M // tm,),
        in_specs=[pl.BlockSpec((tm, D_MODEL), row), _mod_spec(shift, tiles_per_mod),
                  _mod_spec(scale, tiles_per_mod), _full_spec(w1), _full_spec(wcs)],
        out_specs=[pl.BlockSpec((tm, w), row) for w in widths],
        out_shape=[jax.ShapeDtypeStruct((M, w), d) for w, d in zip(widths, dtypes)],
        compiler_params=_params("arbitrary"),
        name="in_proj",
    )(x, shift, scale, w1, wcs)


def _qkv_body(kvin_ref, cq_ref, gq_ref, gkv_ref, wq1_ref, wq2_ref, wkn_ref, wkr_ref, wv_ref,
              qc_ref, qs_ref, kcs_ref, q_ref, k_ref, v_ref):
    nq = _rms(cq_ref[...], gq_ref[...]).astype(BF16)
    q1 = _dot(nq, wq1_ref[...])
    q2 = _dot(nq, wq2_ref[...])
    kvin = kvin_ref[...]
    n = _rms(kvin[:, :KV_LORA], gkv_ref[...]).astype(BF16)
    kr = (kvin[:, KV_LORA:] * kcs_ref[...]).astype(BF16)
    kk = _dot(n, wkn_ref[...]) + _dot(kr, wkr_ref[...])
    v = _dot(n, wv_ref[...])
    qc = qc_ref[...]
    qs = qs_ref[...]
    scale = (QK_NOPE + QK_ROPE) ** -0.5
    for h in range(MLA_HEADS):
        sl = slice(h * HEAD_SLOT, (h + 1) * HEAD_SLOT)
        q_ref[0, h] = ((q1[:, sl] * qc + q2[:, sl] * qs) * scale).astype(BF16)
        k_ref[0, h] = kk[:, sl].astype(BF16)
    for p in range(MLA_HEADS // 2):
        v_ref[0, p] = v[:, p * V7X_LANES:(p + 1) * V7X_LANES].astype(BF16)


def qkv_proj(kvin, cq, gq, gkv, wq1, wq2, wkn, wkr, wv, qc, qs, kcs, B, S):
    tm = TOKEN_TILE
    tps = S // tm
    row = lambda i: (i, 0)
    tab = lambda i: (i % tps, 0)
    seq = lambda i: (i // tps, 0, i % tps, 0)
    return pl.pallas_call(
        _qkv_body,
        grid=(B * tps,),
        in_specs=[pl.BlockSpec((tm, KVIN_WIDTH), row), pl.BlockSpec((tm, Q_LORA), row),
                  _full_spec(gq), _full_spec(gkv), _full_spec(wq1), _full_spec(wq2),
                  _full_spec(wkn), _full_spec(wkr), _full_spec(wv),
                  pl.BlockSpec((tm, HEAD_SLOT), tab), pl.BlockSpec((tm, HEAD_SLOT), tab),
                  pl.BlockSpec((tm, V7X_LANES), tab)],
        out_specs=[pl.BlockSpec((1, MLA_HEADS, tm, HEAD_SLOT), seq),
                   pl.BlockSpec((1, MLA_HEADS, tm, HEAD_SLOT), seq),
                   pl.BlockSpec((1, MLA_HEADS // 2, tm, V7X_LANES), seq)],
        out_shape=[jax.ShapeDtypeStruct((B, MLA_HEADS, S, HEAD_SLOT), BF16),
                   jax.ShapeDtypeStruct((B, MLA_HEADS, S, HEAD_SLOT), BF16),
                   jax.ShapeDtypeStruct((B, MLA_HEADS // 2, S, V7X_LANES), BF16)],
        compiler_params=_params("arbitrary"),
        name="qkv_proj",
    )(kvin, cq, gq, gkv, wq1, wq2, wkn, wkr, wv, qc, qs, kcs)


def _attn_body(*refs, n_sets):
    q_ref = refs[0]
    k_refs = refs[1:1 + 2 * n_sets:2]
    v_refs = refs[2:2 + 2 * n_sets:2]
    o_ref = refs[-1]
    for hp in range(MLA_HEADS // 2):
        outs = []
        for hh in range(2):
            h = 2 * hp + hh
            q = q_ref[0, h]
            s = [lax.dot_general(q, k_ref[0, h], (((1,), (1,)), ((), ())), preferred_element_type=F32)
                 for k_ref in k_refs]
            m = functools.reduce(jnp.maximum, [jnp.max(si, axis=-1, keepdims=True) for si in s])
            p = [jnp.exp(si - m) for si in s]
            l = functools.reduce(jnp.add, [jnp.sum(pi, axis=-1, keepdims=True) for pi in p])
            o = functools.reduce(jnp.add, [_dot(pi.astype(BF16), v_ref[0, hp]) for pi, v_ref in zip(p, v_refs)])
            outs.append(o / l)
        lane = lax.broadcasted_iota(jnp.int32, outs[0].shape, 1)
        o_ref[0, :, hp * V7X_LANES:(hp + 1) * V7X_LANES] = jnp.where(lane < V_DIM, outs[0], outs[1]).astype(BF16)


def attention(q, kv_sets):
    B, H, S, _ = q.shape
    tq = TOKEN_TILE
    in_specs = [pl.BlockSpec((1, H, tq, HEAD_SLOT), lambda b, i: (b, 0, i, 0))]
    args = [q]
    for k, v in kv_sets:
        sk = k.shape[2]
        in_specs.append(pl.BlockSpec((1, H, sk, HEAD_SLOT), lambda b, i: (b, 0, 0, 0)))
        in_specs.append(pl.BlockSpec((1, H // 2, sk, V7X_LANES), lambda b, i: (b, 0, 0, 0)))
        args += [k, v]
    return pl.pallas_call(
        functools.partial(_attn_body, n_sets=len(kv_sets)),
        grid=(B, S // tq),
        in_specs=in_specs,
        out_specs=pl.BlockSpec((1, tq, H * V_DIM), lambda b, i: (b, i, 0)),
        out_shape=jax.ShapeDtypeStruct((B, S, H * V_DIM), BF16),
        compiler_params=_params("arbitrary", "arbitrary"),
        name="attention",
    )(*args)


def _fill_window(win_ref, x_ref, j, n_chunks, tc, halo):
    c = win_ref.shape[1]
    start = pl.multiple_of(j * tc, tc)
    win_ref[halo:halo + tc, :] = x_ref[0, pl.ds(start, tc), :]
    lo = pl.multiple_of(jnp.maximum(j * tc - halo, 0), halo)
    prev = x_ref[0, pl.ds(lo, halo), :]
    win_ref[0:halo, :] = jnp.where(j > 0, prev, jnp.zeros((halo, c), F32))
    hi = pl.multiple_of(jnp.minimum((j + 1) * tc, (n_chunks - 1) * tc + tc - halo), halo)
    nxt = x_ref[0, pl.ds(hi, halo), :]
    win_ref[halo + tc:2 * halo + tc, :] = jnp.where(j < n_chunks - 1, nxt, jnp.zeros((halo, c), F32))


def _conv_taps(win_ref, w_ref, b_ref, out_ref, tc, first_row, n_taps, row_block=64):
    c = win_ref.shape[1]
    for r0 in range(0, tc, row_block):
        for l0 in range(0, c, V7X_LANES):
            ls = slice(l0, l0 + V7X_LANES)
            acc = jnp.broadcast_to(b_ref[:, ls], (row_block, V7X_LANES))
            for k in range(n_taps):
                r = first_row + r0 + k
                acc = acc + win_ref[r:r + row_block, ls] * w_ref[k:k + 1, ls]
            out_ref[r0:r0 + row_block, ls] = acc


CONF_HALO = 16
CONF_ROW_BLOCK = 64


def _conf_body(u_ref, w_ref, b_ref, g_ref, beta_ref, o_ref, win_ref, sh_ref, acc_ref, *, n_chunks, tc):
    j = pl.program_id(1)
    _fill_window(win_ref, u_ref, j, n_chunks, tc, CONF_HALO)
    span = sh_ref.shape[1]
    for s in range(V7X_SUBLANES):
        sh_ref[s] = win_ref[s:s + span, :]
    first_row = CONF_HALO - (CONV_K - 1) // 2
    c = win_ref.shape[1]
    for r0 in range(0, tc, CONF_ROW_BLOCK):
        for l0 in range(0, c, V7X_LANES):
            ls = slice(l0, l0 + V7X_LANES)
            acc = jnp.broadcast_to(b_ref[:, ls], (CONF_ROW_BLOCK, V7X_LANES))
            for k in range(CONV_K):
                phase = (first_row + k) % V7X_SUBLANES
                base = r0 + (first_row + k) // V7X_SUBLANES * V7X_SUBLANES
                acc = acc + sh_ref[phase, base:base + CONF_ROW_BLOCK, ls] * w_ref[k:k + 1, ls]
            acc_ref[r0:r0 + CONF_ROW_BLOCK, ls] = acc
    y = _ln(acc_ref[...]) * g_ref[...] + beta_ref[...]
    o_ref[0] = (y * _sigmoid(y)).astype(BF16)


def conformer(u, w, b, g, beta):
    B, S, C = u.shape
    tc = TOKEN_TILE
    n_chunks = S // tc
    span = tc + 2 * CONF_HALO - V7X_SUBLANES
    return pl.pallas_call(
        functools.partial(_conf_body, n_chunks=n_chunks, tc=tc),
        grid=(B, n_chunks),
        in_specs=[pl.BlockSpec((1, S, C), lambda bb, j: (bb, 0, 0)),
                  _full_spec(w), _full_spec(b), _full_spec(g), _full_spec(beta)],
        out_specs=pl.BlockSpec((1, tc, C), lambda bb, j: (bb, j, 0)),
        out_shape=jax.ShapeDtypeStruct((B, S, C), BF16),
        scratch_shapes=[pltpu.VMEM((tc + 2 * CONF_HALO, C), F32), pltpu.VMEM((V7X_SUBLANES, span, C), F32),
                        pltpu.VMEM((tc, C), F32)],
        compiler_params=_params("arbitrary", "arbitrary"),
        name="conformer",
    )(u, w, b, g, beta)


def _fnet_body(c_ref, s_ref, a_ref, ar_ref, b_ref, br_ref, amid_ref, o_ref, *, norm, ts):
    a_fold = (a_ref[0].astype(F32) + ar_ref[0].astype(F32)).astype(BF16)
    b_fold = (b_ref[0].astype(F32) - br_ref[0].astype(F32)).astype(BF16)
    y = _dot(c_ref[...], a_fold) + _dot(s_ref[...], b_fold)
    j = lax.broadcasted_iota(jnp.int32, (ts, 1), 0) + pl.program_id(0) * ts
    sign = (1 - 2 * (j & 1)).astype(F32)
    y = y + sign * amid_ref[0].astype(F32)
    o_ref[0] = (y * norm).astype(BF16)


def fnet_seq_dft(cmat, smat, fa, fb):
    B, S, W = fa.shape
    half = S // 2
    ts = min(S, 1024)
    norm = 1.0 / math.sqrt(S * (FNET_WIDTH // FNET_GROUPS))

    def mirrored(t):
        return jnp.concatenate([jnp.zeros((B, 1, W), t.dtype), jnp.flip(t[:, half + 1:], axis=1)], axis=1)

    first_half = pl.BlockSpec((1, half, W), lambda i, b: (b, 0, 0))
    return pl.pallas_call(
        functools.partial(_fnet_body, norm=norm, ts=ts),
        grid=(S // ts, B),
        in_specs=[pl.BlockSpec((ts, half), lambda i, b: (i, 0)), pl.BlockSpec((ts, half), lambda i, b: (i, 0)),
                  first_half, first_half, first_half, first_half,
                  pl.BlockSpec((1, 1, W), lambda i, b: (b, 0, 0))],
        out_specs=pl.BlockSpec((1, ts, W), lambda i, b: (b, i, 0)),
        out_shape=jax.ShapeDtypeStruct((B, S, W), BF16),
        compiler_params=_params("arbitrary", "arbitrary"),
        name="fnet_seq_dft",
    )(cmat, smat, fa, mirrored(fa), fb, mirrored(fb), fa[:, half:half + 1])


LRU_HALO = 8
LRU_GROUP = V7X_SUBLANES
LRU_UNROLL = 8


def _lru_body(*refs, n_chunks, tc, reverse):
    if reverse:
        (x_ref, cw_ref, cb_ref, wg_ref, bg_ref, ca_ref, h0_ref, hf_ref, gb_ref,
         o_ref, s_ref, win_ref, xc_ref, a_ref, u_ref, carry_ref) = refs
    else:
        (x_ref, cw_ref, cb_ref, wg_ref, bg_ref, ca_ref, h0_ref,
         o_ref, s_ref, win_ref, xc_ref, a_ref, u_ref, carry_ref) = refs
    step = pl.program_id(1)
    j = (n_chunks - 1 - step) if reverse else step

    @pl.when(step == 0)
    def _():
        carry_ref[...] = jnp.broadcast_to(h0_ref[0], carry_ref.shape)

    _fill_window(win_ref, x_ref, j, n_chunks, tc, LRU_HALO)
    _conv_taps(win_ref, cw_ref, cb_ref, xc_ref, tc, LRU_HALO - LRU_CONV_K // 2, LRU_CONV_K)
    xc = xc_ref[...]
    gates = _dot(xc.astype(BF16), wg_ref[...]) + bg_ref[...]
    r = _sigmoid(gates[:, :LRU_WIDTH])
    i = _sigmoid(gates[:, LRU_WIDTH:])
    log_a = ca_ref[...] * r
    a = jnp.exp(log_a)
    a_ref[...] = a
    u_ref[...] = jnp.sqrt(-jnp.tanh(log_a) * (a * a + 1.0)) * (i * xc)

    n_groups = tc // LRU_GROUP
    row = lax.broadcasted_iota(jnp.int32, (LRU_GROUP, LRU_WIDTH), 0)

    def group(gi, carry):
        g = (n_groups - 1 - gi) if reverse else gi
        rows = pl.ds(pl.multiple_of(g * LRU_GROUP, LRU_GROUP), LRU_GROUP)
        a = a_ref[rows, :]
        u = u_ref[rows, :]
        for sh in (1, 2, 4):
            if reverse:
                a_sh = pltpu.roll(a, LRU_GROUP - sh, 0)
                u_sh = pltpu.roll(u, LRU_GROUP - sh, 0)
                valid = row < LRU_GROUP - sh
            else:
                a_sh = pltpu.roll(a, sh, 0)
                u_sh = pltpu.roll(u, sh, 0)
                valid = row >= sh
            a_sh = jnp.where(valid, a_sh, 1.0)
            u_sh = jnp.where(valid, u_sh, 0.0)
            u = a * u_sh + u
            a = a * a_sh
        h = u + a * carry
        u_ref[rows, :] = h
        last = h[0:1] if reverse else h[LRU_GROUP - 1:LRU_GROUP]
        return jnp.broadcast_to(last, h.shape)

    carry = lax.fori_loop(0, n_groups, group, carry_ref[...], unroll=LRU_UNROLL)
    carry_ref[...] = carry
    h = u_ref[...]
    if reverse:
        gb = gb_ref[0]
        gelu = 0.5 * gb * (1.0 + jnp.tanh(math.sqrt(2.0 / math.pi) * (gb + 0.044715 * (gb * gb * gb))))
        o_ref[0] = ((hf_ref[0] + h) * gelu).astype(BF16)
    else:
        o_ref[0] = h

    @pl.when(step == n_chunks - 1)
    def _():
        s_ref[0] = carry[0:1]


def lru_scan(xb, cw, cb, wg, bg, ca, h0, hf=None, gb=None):
    B, S, W = xb.shape
    reverse = hf is not None
    tc = TOKEN_TILE
    n_chunks = S // tc
    chunk = (lambda bb, s: (bb, n_chunks - 1 - s, 0)) if reverse else (lambda bb, s: (bb, s, 0))
    per_b = lambda bb, s: (bb, 0, 0)
    in_specs = [pl.BlockSpec((1, S, W), per_b), _full_spec(cw), _full_spec(cb), _full_spec(wg),
                _full_spec(bg), _full_spec(ca), pl.BlockSpec((1, 1, W), per_b)]
    args = [xb, cw, cb, wg, bg, ca, h0]
    if reverse:
        in_specs += [pl.BlockSpec((1, tc, W), chunk), pl.BlockSpec((1, tc, W), chunk)]
        args += [hf, gb]
    return pl.pallas_call(
        functools.partial(_lru_body, n_chunks=n_chunks, tc=tc, reverse=reverse),
        grid=(B, n_chunks),
        in_specs=in_specs,
        out_specs=[pl.BlockSpec((1, tc, W), chunk), pl.BlockSpec((1, 1, W), per_b)],
        out_shape=[jax.ShapeDtypeStruct((B, S, W), BF16 if reverse else F32),
                   jax.ShapeDtypeStruct((B, 1, W), F32)],
        scratch_shapes=[pltpu.VMEM((tc + 2 * LRU_HALO, W), F32), pltpu.VMEM((tc, W), F32),
                        pltpu.VMEM((tc, W), F32), pltpu.VMEM((tc, W), F32),
                        pltpu.VMEM((LRU_GROUP, W), F32)],
        compiler_params=_params("arbitrary", "arbitrary"),
        name="lru_bwd" if reverse else "lru_fwd",
    )(*args)


def _split_bf16(x):
    hi = x.astype(BF16)
    return hi, (x - hi.astype(F32)).astype(BF16)


def _merge_body(x_ref, sh1_ref, sc1_ref, g1_ref, sh2_ref, sc2_ref, attn_ref, conf_ref, fnet_ref, lru_ref,
                wgl_ref, wb_ref, wout_ref, lng_ref, lnb_ref, wrh_ref, wrl_ref,
                y_ref, h2_ref, aff_ref):
    x = x_ref[...]
    h = (_ln(x) * (1.0 + sc1_ref[0]) + sh1_ref[0]).astype(BF16)
    acc = None
    for k, br_ref in enumerate((attn_ref, conf_ref, fnet_ref, lru_ref)):
        gl = _dot(h, wgl_ref[:, k * D_MODEL:(k + 1) * D_MODEL])
        term = _sigmoid(gl) * _dot(br_ref[...], wb_ref[k])
        acc = term if acc is None else acc + term
    out = _dot(acc.astype(BF16), wout_ref[...])
    y = _ln(DEEPNORM_ALPHA * x + g1_ref[0] * out) * lng_ref[...] + lnb_ref[...]
    y_ref[...] = y
    h2 = _ln(y) * (1.0 + sc2_ref[0]) + sh2_ref[0]
    h2_ref[...] = h2.astype(BF16)
    hh, hl = _split_bf16(h2)
    logits = _dot(hh, wrh_ref[...]) + _dot(hh, wrl_ref[...]) + _dot(hl, wrh_ref[...])
    lane = lax.broadcasted_iota(jnp.int32, logits.shape, 1)
    valid = lane < N_EXPERTS
    logits = jnp.where(valid, logits, -jnp.inf)
    e = jnp.exp(logits - jnp.max(logits, axis=-1, keepdims=True))
    aff_ref[...] = e / jnp.sum(e, axis=-1, keepdims=True)


def merge(x, mods, branches, wgl, wb, wout, lng, lnb, wrh, wrl, tiles_per_mod):
    M = x.shape[0]
    tm = TOKEN_TILE
    row = lambda i: (i, 0)
    sh1, sc1, g1, sh2, sc2 = mods
    in_specs = ([pl.BlockSpec((tm, D_MODEL), row)] + [_mod_spec(m, tiles_per_mod) for m in mods]
                + [pl.BlockSpec((tm, BRANCH_WIDTH), row)] * N_BRANCH
                + [_full_spec(a) for a in (wgl, wb, wout, lng, lnb, wrh, wrl)])
    return pl.pallas_call(
        _merge_body,
        grid=(M // tm,),
        in_specs=in_specs,
        out_specs=[pl.BlockSpec((tm, D_MODEL), row), pl.BlockSpec((tm, D_MODEL), row),
                   pl.BlockSpec((tm, V7X_LANES), row)],
        out_shape=[jax.ShapeDtypeStruct((M, D_MODEL), F32), jax.ShapeDtypeStruct((M, D_MODEL), BF16),
                   jax.ShapeDtypeStruct((M, V7X_LANES), F32)],
        compiler_params=_params("arbitrary"),
        name="merge",
    )(x, sh1, sc1, g1, sh2, sc2, *branches, wgl, wb, wout, lng, lnb, wrh, wrl)


def _select_body(aff_ref, tri_ref, pos_ref, rng_ref, *, cap, n_tok, sub, blk):
    bits = pltpu.bitcast(aff_ref[0], jnp.int32)

    def bisect(it, lo):
        cand = lo | jnp.left_shift(jnp.int32(1), 30 - it)
        cnt = jnp.sum(jnp.where(bits >= cand, 1.0, 0.0), axis=1, keepdims=True)
        return jnp.where(cnt >= cap, cand, lo)

    thr = lax.fori_loop(0, 31, bisect, jnp.zeros((N_EXPERTS, 1), jnp.int32))
    need = cap - jnp.sum(jnp.where(bits > thr, 1.0, 0.0), axis=1, keepdims=True)
    tri = tri_ref[...]
    ties_before = jnp.zeros((N_EXPERTS, 1), F32)
    sel_before = jnp.zeros((N_EXPERTS, 1), F32)
    for c0 in range(0, n_tok, V7X_LANES):
        sl = slice(c0, c0 + V7X_LANES)
        bits_c = bits[:, sl]
        gt_c = jnp.where(bits_c > thr, 1.0, 0.0)
        eq_c = jnp.where(bits_c == thr, 1.0, 0.0)
        tie_rank = _dot(eq_c.astype(BF16), tri) + ties_before
        ties_before = ties_before + jnp.sum(eq_c, axis=1, keepdims=True)
        sel_c = gt_c + eq_c * jnp.where(tie_rank < need, 1.0, 0.0)
        slot = _dot(sel_c.astype(BF16), tri) + sel_before
        sel_before = sel_before + jnp.sum(sel_c, axis=1, keepdims=True)
        pos_ref[0, :, sl] = jnp.where(sel_c > 0.0, slot, -1.0)

    pos = pos_ref[0]
    tok = lax.broadcasted_iota(jnp.int32, pos.shape, 1).astype(F32)
    lane = lax.broadcasted_iota(jnp.int32, (N_EXPERTS, V7X_LANES), 1)
    rng = jnp.zeros((N_EXPERTS, V7X_LANES), F32)
    for q in range(cap // sub):
        in_group = lambda val, fill: jnp.where(pos >= q * sub, jnp.where(pos < (q + 1) * sub, val, fill), fill)
        first = jnp.min(in_group(tok, float(n_tok)), axis=1, keepdims=True)
        last = jnp.max(in_group(tok, -1.0), axis=1, keepdims=True)
        rng = jnp.where(lane == 2 * q, jnp.floor(first * (1.0 / blk)), rng)
        rng = jnp.where(lane == 2 * q + 1, jnp.floor(last * (1.0 / blk)), rng)
    rng_ref[0] = rng


def select_tokens(aff_t, tri, cap, sub, blk):
    B, E, N = aff_t.shape
    return pl.pallas_call(
        functools.partial(_select_body, cap=cap, n_tok=N, sub=sub, blk=blk),
        grid=(B,),
        in_specs=[pl.BlockSpec((1, E, N), lambda b: (b, 0, 0)), _full_spec(tri)],
        out_specs=[pl.BlockSpec((1, E, N), lambda b: (b, 0, 0)),
                   pl.BlockSpec((1, E, V7X_LANES), lambda b: (b, 0, 0))],
        out_shape=[jax.ShapeDtypeStruct((B, E, N), F32), jax.ShapeDtypeStruct((B, E, V7X_LANES), F32)],
        compiler_params=_params("arbitrary"),
        name="select_tokens",
    )(aff_t, tri)


def _experts_body(rng_ref, pos_ref, h_ref, wg_ref, wu_ref, wd_ref, y_ref, xe_ref, *, sub, blk, n_sub):
    e = pl.program_id(0)
    b = pl.program_id(1)
    base = (b * N_EXPERTS + e) * (2 * n_sub)
    for q in range(n_sub):
        rows = slice(q * sub, (q + 1) * sub)
        slot = lax.broadcasted_iota(jnp.int32, (sub, 1), 0) + q * sub
        xe_ref[rows, :] = jnp.zeros((sub, D_MODEL), F32)

        def gather_block(tb, carry, rows=rows, slot=slot):
            pos = pos_ref[0, tb].astype(jnp.int32)
            onehot = jnp.where(slot == pos, 1.0, 0.0).astype(BF16)
            tok = pl.ds(pl.multiple_of(tb * blk, blk), blk)
            xe_ref[rows, :] += _dot(onehot, h_ref[tok, :])
            return carry

        lax.fori_loop(rng_ref[base + 2 * q], rng_ref[base + 2 * q + 1] + 1, gather_block, 0)
    xe = xe_ref[...].astype(BF16)
    g = _dot(xe, wg_ref[0])
    u = _dot(xe, wu_ref[0])
    he = (g * _sigmoid(g) * u).astype(BF16)
    y_ref[0, 0] = _dot(he, wd_ref[0]).astype(BF16)


def experts(ranges, pos_blocks, h2, wg, wu, wd, B, N, cap, sub, blk):
    E = N_EXPERTS
    n_blk = N // blk
    grid_spec = pltpu.PrefetchScalarGridSpec(
        num_scalar_prefetch=1,
        grid=(E, B),
        in_specs=[pl.BlockSpec((1, n_blk, 1, blk), lambda e, b, r: (b * E + e, 0, 0, 0)),
                  pl.BlockSpec((N, D_MODEL), lambda e, b, r: (b, 0)),
                  pl.BlockSpec((1, D_MODEL, EXPERT_FF), lambda e, b, r: (e, 0, 0)),
                  pl.BlockSpec((1, D_MODEL, EXPERT_FF), lambda e, b, r: (e, 0, 0)),
                  pl.BlockSpec((1, EXPERT_FF, D_MODEL), lambda e, b, r: (e, 0, 0))],
        out_specs=pl.BlockSpec((1, 1, cap, D_MODEL), lambda e, b, r: (b, e, 0, 0)),
        scratch_shapes=[pltpu.VMEM((cap, D_MODEL), F32)])
    return pl.pallas_call(
        functools.partial(_experts_body, sub=sub, blk=blk, n_sub=cap // sub),
        grid_spec=grid_spec,
        out_shape=jax.ShapeDtypeStruct((B, E, cap, D_MODEL), BF16),
        compiler_params=_params("arbitrary", "arbitrary"),
        name="experts",
    )(ranges, pos_blocks, h2, wg, wu, wd)


def _combine_body(x_ref, g2_ref, posc_ref, aff_ref, ye_ref, lng_ref, lnb_ref, o_ref, *, cap):
    posc = posc_ref[0]
    aff = aff_ref[...]
    slot = lax.broadcasted_iota(jnp.int32, (1, cap), 1).astype(F32)
    acc = None
    for e in range(N_EXPERTS):
        pcol = posc[:, e:e + 1]
        onehot = jnp.where(pcol == slot, 1.0, 0.0).astype(BF16)
        w = jnp.where(pcol >= 0.0, aff[:, e:e + 1], 0.0)
        term = w * _dot(onehot, ye_ref[0, e])
        acc = term if acc is None else acc + term
    y = _ln(DEEPNORM_ALPHA * x_ref[...] + g2_ref[0] * acc)
    o_ref[...] = y * lng_ref[...] + lnb_ref[...]


def combine(x, gate2, posc, aff, ye, lng, lnb, B, N, cap):
    tm = min(N, 512)
    tpb = N // tm
    row = lambda i: (i, 0)
    gate_spec = (pl.BlockSpec((1, 1, D_MODEL), lambda i: (0, 0, 0)) if gate2.shape[0] == 1
                 else pl.BlockSpec((1, 1, D_MODEL), lambda i: (i // tpb, 0, 0)))
    return pl.pallas_call(
        functools.partial(_combine_body, cap=cap),
        grid=(B * tpb,),
        in_specs=[pl.BlockSpec((tm, D_MODEL), row), gate_spec,
                  pl.BlockSpec((1, tm, V7X_LANES), lambda i: (i // tpb, i % tpb, 0)),
                  pl.BlockSpec((tm, V7X_LANES), row),
                  pl.BlockSpec((1, N_EXPERTS, cap, D_MODEL), lambda i: (i // tpb, 0, 0, 0)),
                  _full_spec(lng), _full_spec(lnb)],
        out_specs=pl.BlockSpec((tm, D_MODEL), row),
        out_shape=jax.ShapeDtypeStruct((B * N, D_MODEL), F32),
        compiler_params=_params("arbitrary"),
        name="combine",
    )(x, gate2, posc, aff, ye, lng, lnb)


def _rope_rotation():
    r = [[0.0] * QK_ROPE for _ in range(QK_ROPE)]
    half = QK_ROPE // 4
    for base in (0, QK_ROPE // 2):
        for j in range(half):
            r[base + j + half][base + j] = -1.0
            r[base + j][base + j + half] = 1.0
    return jnp.array(r, F32)


def _rope_tables(rows):
    row = jnp.repeat(jnp.arange(rows), GRID_W).astype(F32)
    col = jnp.tile(jnp.arange(GRID_W), rows).astype(F32)
    inv = ROPE_BASE ** (-jnp.arange(ROPE_AXIS_FREQ, dtype=F32) / ROPE_AXIS_FREQ)
    ang_r = row[:, None] * inv
    ang_c = col[:, None] * inv
    cos = jnp.concatenate([jnp.cos(ang_r)] * 2 + [jnp.cos(ang_c)] * 2, axis=-1)
    sin = jnp.concatenate([jnp.sin(ang_r)] * 2 + [jnp.sin(ang_c)] * 2, axis=-1)
    return cos, sin


def _position_tables(cos, sin):
    S = cos.shape[0]
    ones = jnp.ones((S, QK_NOPE), F32)
    pad = jnp.zeros((S, HEAD_SLOT - QK_NOPE - QK_ROPE), F32)
    qc = jnp.concatenate([ones, cos, pad], axis=-1)
    qs = jnp.concatenate([jnp.zeros((S, QK_NOPE), F32), sin, pad], axis=-1)
    kcs = jnp.concatenate([cos, sin, jnp.zeros((S, V7X_LANES - 2 * QK_ROPE), F32)], axis=-1)
    return qc, qs, kcs


def _dft_matrices(S):
    j = jnp.arange(S, dtype=jnp.int32)
    k = jnp.arange(S // 2, dtype=jnp.int32)
    ang = ((j[:, None] * k[None, :]) % S).astype(F32) * (2.0 * math.pi / S)
    return jnp.cos(ang).astype(BF16), (-jnp.sin(ang)).astype(BF16)


def _channel_dft_weights():
    n = FNET_WIDTH // FNET_GROUPS
    k = jnp.arange(n, dtype=jnp.int32)
    ang = ((k[:, None] * k[None, :]) % n).astype(F32) * (2.0 * math.pi / n)
    eye = jnp.eye(FNET_GROUPS, dtype=F32)
    return jnp.concatenate([jnp.kron(eye, jnp.cos(ang)), jnp.kron(eye, jnp.sin(ang))], axis=1).astype(BF16)


def _block_diag_heads(w):
    H, d, _ = w.shape
    out = jnp.zeros((H, d, H, d), w.dtype)
    out = out.at[jnp.arange(H), :, jnp.arange(H), :].set(w)
    return out.reshape(H * d, H * d)


def _layer_weights(l, w_in, q_norm_g, w_uq, kv_norm_g, w_ukv, lru_wa, lru_ba, lru_wx, lru_bx, lru_lambda,
                   w_branch, w_out, w_router, rot):
    wi = w_in[l]
    o = 0
    ckv = wi[:, o:o + KV_LORA]
    o += KV_LORA
    kr = wi[:, o:o + QK_ROPE]
    o += QK_ROPE
    xb = wi[:, o:o + LRU_WIDTH]
    o += LRU_WIDTH
    cq = wi[:, o:o + Q_LORA]
    o += Q_LORA
    glu = wi[:, o:o + 2 * CONV_WIDTH]
    o += 2 * CONV_WIDTH
    f = wi[:, o:o + FNET_WIDTH]
    o += FNET_WIDTH
    gb = wi[:, o:o + LRU_WIDTH]
    o += LRU_WIDTH
    gl = wi[:, o:]
    kpad = jnp.zeros((D_MODEL, V7X_LANES - 2 * QK_ROPE), F32)
    w1 = jnp.concatenate([ckv, kr, kr @ rot, kpad, xb, cq, glu, f, gb], axis=1).astype(BF16)

    uq = w_uq[l]
    zq = jnp.zeros((Q_LORA, MLA_HEADS, HEAD_SLOT - QK_NOPE - QK_ROPE), F32)
    wq1 = jnp.concatenate([uq, zq], axis=-1).reshape(Q_LORA, MLA_HEADS * HEAD_SLOT).astype(BF16)
    uq_rot = jnp.einsum('rhd,de->rhe', uq[..., QK_NOPE:], rot)
    wq2 = jnp.concatenate([jnp.zeros((Q_LORA, MLA_HEADS, QK_NOPE), F32), uq_rot, zq], axis=-1)
    wq2 = wq2.reshape(Q_LORA, MLA_HEADS * HEAD_SLOT).astype(BF16)
    ukv = w_ukv[l]
    zk = jnp.zeros((KV_LORA, MLA_HEADS, HEAD_SLOT - QK_NOPE), F32)
    wkn = jnp.concatenate([ukv[..., :QK_NOPE], zk], axis=-1).reshape(KV_LORA, MLA_HEADS * HEAD_SLOT).astype(BF16)
    wv = ukv[..., QK_NOPE:].reshape(KV_LORA, MLA_HEADS * V_DIM).astype(BF16)
    eye = jnp.eye(QK_ROPE, dtype=F32)
    slot = jnp.concatenate([jnp.zeros((QK_ROPE, QK_NOPE), F32), eye,
                            jnp.zeros((QK_ROPE, HEAD_SLOT - QK_NOPE - QK_ROPE), F32)], axis=1)
    slot = jnp.tile(slot, (1, MLA_HEADS))
    wkr = jnp.concatenate([slot, slot, jnp.zeros((V7X_LANES - 2 * QK_ROPE, MLA_HEADS * HEAD_SLOT), F32)],
                          axis=0).astype(BF16)

    lru = []
    for d in range(2):
        wg = jnp.concatenate([_block_diag_heads(lru_wa[l, d]), _block_diag_heads(lru_wx[l, d])], axis=1).astype(BF16)
        bg = jnp.concatenate([lru_ba[l, d], lru_bx[l, d]])[None, :]
        ca = (-LRU_C * jax.nn.softplus(-lru_lambda[l, d]))[None, :]
        lru.append((wg, bg, ca))

    wr = jnp.pad(w_router[l], ((0, 0), (0, V7X_LANES - N_EXPERTS)))
    wrh, wrl = _split_bf16(wr)
    return dict(w1=w1, wq1=wq1, wq2=wq2, wkn=wkn, wkr=wkr, wv=wv, gq=q_norm_g[l][None, :],
                gkv=kv_norm_g[l][None, :], lru=lru, wgl=gl.astype(BF16), wb=w_branch[l].astype(BF16),
                wout=w_out[l].astype(BF16), wrh=wrh, wrl=wrl)


def _sequence_mixers(B, S, proj, lw, cv, lru_conv, tabs, dft, h0, need_branches=True):
    kvin, xb, cq, u, fa, fb, gb = proj
    q, k, v = qkv_proj(kvin, cq, lw['gq'], lw['gkv'], lw['wq1'], lw['wq2'], lw['wkn'], lw['wkr'], lw['wv'],
                       *tabs, B, S)
    conf = fnet = None
    if need_branches:
        conf = conformer(u.reshape(B, S, CONV_WIDTH), *cv)
        fnet = fnet_seq_dft(*dft, fa.reshape(B, S, FNET_WIDTH), fb.reshape(B, S, FNET_WIDTH))
    xb3 = xb.reshape(B, S, LRU_WIDTH)
    hf, s_fwd = lru_scan(xb3, *lru_conv, *lw['lru'][0], h0[0])
    hrec, s_bwd = lru_scan(xb3, *lru_conv, *lw['lru'][1], h0[1], hf=hf, gb=gb.reshape(B, S, LRU_WIDTH))
    return q, k, v, conf, fnet, hrec, (s_fwd, s_bwd)


def _moe(y, h2, aff, gate2, lw_e, ln2, tri, B, N):
    cap = EC_CAPACITY * N // N_EXPERTS
    aff_t = jnp.swapaxes(aff.reshape(B, N, V7X_LANES)[:, :, :N_EXPERTS], 1, 2)
    sub = min(cap, EXPERT_SLOT_GROUP)
    blk = EXPERT_TOKEN_BLOCK
    pos, rng = select_tokens(aff_t, tri, cap, sub, blk)
    ranges = rng[:, :, :2 * (cap // sub)].astype(jnp.int32).reshape(-1)
    ye = experts(ranges, pos.reshape(B * N_EXPERTS, N // blk, 1, blk), h2, *lw_e, B, N, cap, sub, blk)
    posc = jnp.pad(jnp.swapaxes(pos, 1, 2), ((0, 0), (0, 0), (0, V7X_LANES - N_EXPERTS)), constant_values=-1.0)
    return combine(y, gate2, posc, aff, ye, *ln2, B, N, cap)


def kernel(x, c, ctx, c_ctx, ada_w, ada_b, w_in, q_norm_g, w_uq, kv_norm_g, w_ukv, cv_w, cv_b, cv_ln_g, cv_ln_b, lru_conv_w, lru_conv_b, lru_wa, lru_ba, lru_wx, lru_bx, lru_lambda, w_branch, w_out, ln1_g, ln1_b, w_router, w_e_gate, w_e_up, w_e_down, ln2_g, ln2_b):
    B, S, _ = x.shape
    SC = ctx.shape[1]
    rot = _rope_rotation()
    cos, sin = _rope_tables(S // GRID_W)
    tabs_l = _position_tables(cos, sin)
    tabs_c = _position_tables(jnp.ones((SC, QK_ROPE), F32), jnp.zeros((SC, QK_ROPE), F32))
    dft_l = _dft_matrices(S)
    dft_c = _dft_matrices(SC)
    wcs = _channel_dft_weights()
    tri = jnp.triu(jnp.ones((V7X_LANES, V7X_LANES), F32), k=1).astype(BF16)
    zero_state = jnp.zeros((B, 1, LRU_WIDTH), F32)

    x_lat = x.reshape(B * S, D_MODEL)
    x_ctx = ctx.reshape(B * SC, D_MODEL)
    tpm_l = S // TOKEN_TILE
    tpm_c = SC // TOKEN_TILE
    for l in range(DEPTH):
        ctx_out = l < DEPTH - 1
        lw = _layer_weights(l, w_in, q_norm_g, w_uq, kv_norm_g, w_ukv, lru_wa, lru_ba, lru_wx, lru_bx,
                            lru_lambda, w_branch, w_out, w_router, rot)
        cv = (cv_w[l], cv_b[l][None, :], cv_ln_g[l][None, :], cv_ln_b[l][None, :])
        lru_conv = (lru_conv_w[l], lru_conv_b[l][None, :])
        ln1 = (ln1_g[l][None, :], ln1_b[l][None, :])
        ln2 = (ln2_g[l][None, :], ln2_b[l][None, :])
        lw_e = (w_e_gate[l].astype(BF16), w_e_up[l].astype(BF16), w_e_down[l].astype(BF16))

        mod = jax.nn.silu(c) @ ada_w[l] + ada_b[l]
        mods_l = [m[:, None, :] for m in jnp.split(mod, 6, axis=-1)]
        mod_c = jax.nn.silu(c_ctx) @ ada_w[l] + ada_b[l]
        mods_c = [m[None, None, :] for m in jnp.split(mod_c, 6)]

        proj_c = in_proj(x_ctx, mods_c[0], mods_c[1], lw['w1'], wcs, SC)
        q_c, k_c, v_c, conf_c, fnet_c, hrec_c, states = _sequence_mixers(
            B, SC, proj_c, lw, cv, lru_conv, tabs_c, dft_c, (zero_state, zero_state), need_branches=ctx_out)
        proj_l = in_proj(x_lat, mods_l[0], mods_l[1], lw['w1'], wcs, S)
        q_l, k_l, v_l, conf_l, fnet_l, hrec_l, _ = _sequence_mixers(
            B, S, proj_l, lw, cv, lru_conv, tabs_l, dft_l, states)

        attn_l = attention(q_l, [(k_l, v_l), (k_c, v_c)])
        merge_w = (lw['wgl'], lw['wb'], lw['wout'], *ln1, lw['wrh'], lw['wrl'])
        sel = lambda ms: (ms[0], ms[1], ms[2], ms[3], ms[4])
        y_l, h2_l, aff_l = merge(x_lat, sel(mods_l),
                                 [attn_l.reshape(B * S, -1), conf_l.reshape(B * S, -1),
                                  fnet_l.reshape(B * S, -1), hrec_l.reshape(B * S, -1)], *merge_w, tpm_l)
        x_lat = _moe(y_l, h2_l, aff_l, mods_l[5], lw_e, ln2, tri, B, S)
        if ctx_out:
            attn_c = attention(q_c, [(k_c, v_c)])
            y_c, h2_c, aff_c = merge(x_ctx, sel(mods_c),
                                     [attn_c.reshape(B * SC, -1), conf_c.reshape(B * SC, -1),
                                      fnet_c.reshape(B * SC, -1), hrec_c.reshape(B * SC, -1)], *merge_w, tpm_c)
            x_ctx = _moe(y_c, h2_c, aff_c, mods_c[5], lw_e, ln2, tri, B, SC)
    return x_lat.reshape(B, S, D_MODEL)
```

```python
import functools
import math

import jax
import jax.numpy as jnp
from jax import lax
from jax.experimental import pallas as pl
from jax.experimental.pallas import tpu as pltpu

D_MODEL = 1024
DEPTH = 4
GRID_W = 64
MLA_HEADS = 8
Q_LORA = 384
KV_LORA = 256
QK_NOPE = 64
QK_ROPE = 32
V_DIM = 64
ROPE_AXIS_FREQ = QK_ROPE // 4
ROPE_BASE = 10000.0
CONV_WIDTH = 512
CONV_K = 31
FNET_WIDTH = 512
FNET_GROUPS = 4
LRU_WIDTH = 512
LRU_HEADS = 8
LRU_CONV_K = 4
LRU_C = 8.0
N_BRANCH = 4
BRANCH_WIDTH = 512
N_EXPERTS = 16
EXPERT_FF = 1408
EC_CAPACITY = 2
DEEPNORM_ALPHA = (2 * DEPTH) ** 0.25
LN_EPS = 1e-6

V7X_LANES = 128
V7X_SUBLANES = 8
V7X_VMEM_LIMIT_BYTES = 56 * 1024 * 1024
HEAD_SLOT = V7X_LANES
TOKEN_TILE = 256
IN_PROJ_TILE = 512
EXPERT_TOKEN_BLOCK = 256
COMBINE_TOKEN_BLOCK = 128
EXPERT_ALIGN_LOG2 = 4
KVIN_WIDTH = KV_LORA + V7X_LANES
IN_PROJ_WIDTH = KVIN_WIDTH + LRU_WIDTH + Q_LORA + 2 * CONV_WIDTH + FNET_WIDTH + LRU_WIDTH

F32 = jnp.float32
BF16 = jnp.bfloat16


def _params(*sem):
    return pltpu.CompilerParams(dimension_semantics=sem, vmem_limit_bytes=V7X_VMEM_LIMIT_BYTES)


def _sigmoid(x):
    return 1.0 / (1.0 + jnp.exp(-x))


def _ln(x):
    mu = jnp.mean(x, axis=-1, keepdims=True)
    xc = x - mu
    var = jnp.mean(xc * xc, axis=-1, keepdims=True)
    return xc * lax.rsqrt(var + LN_EPS)


def _rms(x, g):
    return x * lax.rsqrt(jnp.mean(x * x, axis=-1, keepdims=True) + LN_EPS) * g


def _dot(a, b):
    return jnp.dot(a, b, preferred_element_type=F32)


def _mod_spec(mods, tiles_per_mod):
    if mods.shape[0] == 1:
        return pl.BlockSpec((1, 1, D_MODEL), lambda i: (0, 0, 0))
    return pl.BlockSpec((1, 1, D_MODEL), lambda i: (i // tiles_per_mod, 0, 0))


def _full_spec(arr):
    zeros = (0,) * arr.ndim
    return pl.BlockSpec(arr.shape, lambda *_: zeros)


def _in_proj_body(x_ref, sh_ref, sc_ref, w_ref, wcs_ref,
                  kvin_ref, xb_ref, cq_ref, u_ref, fa_ref, fb_ref, gb_ref):
    h = _ln(x_ref[...]) * (1.0 + sc_ref[0]) + sh_ref[0]
    p = _dot(h.astype(BF16), w_ref[...])
    o = 0
    kvin_ref[...] = p[:, o:o + KVIN_WIDTH]
    o += KVIN_WIDTH
    xb_ref[...] = p[:, o:o + LRU_WIDTH]
    o += LRU_WIDTH
    cq_ref[...] = p[:, o:o + Q_LORA]
    o += Q_LORA
    a = p[:, o:o + CONV_WIDTH]
    g = p[:, o + CONV_WIDTH:o + 2 * CONV_WIDTH]
    u_ref[...] = a * _sigmoid(g)
    o += 2 * CONV_WIDTH
    f = p[:, o:o + FNET_WIDTH].astype(BF16)
    o += FNET_WIDTH
    gb_ref[...] = p[:, o:o + LRU_WIDTH]
    ab = _dot(f, wcs_ref[...])
    fa_ref[...] = ab[:, :FNET_WIDTH].astype(BF16)
    fb_ref[...] = ab[:, FNET_WIDTH:].astype(BF16)


def in_proj(x, shift, scale, w1, wcs, rows_per_mod):
    M = x.shape[0]
    tm = min(IN_PROJ_TILE, rows_per_mod)
    tiles_per_mod = rows_per_mod // tm
    row = lambda i: (i, 0)
    widths = (KVIN_WIDTH, LRU_WIDTH, Q_LORA, CONV_WIDTH, FNET_WIDTH, FNET_WIDTH, LRU_WIDTH)
    dtypes = (F32, F32, F32, F32, BF16, BF16, F32)
    return pl.pallas_call(
        _in_proj_body,
        grid=(M // tm,),
        in_specs=[pl.BlockSpec((tm, D_MODEL), row), _mod_spec(shift, tiles_per_mod),
                  _mod_spec(scale, tiles_per_mod), _full_spec(w1), _full_spec(wcs)],
        out_specs=[pl.BlockSpec((tm, w), row) for w in widths],
        out_shape=[jax.ShapeDtypeStruct((M, w), d) for w, d in zip(widths, dtypes)],
        compiler_params=_params("arbitrary"),
        name="in_proj",
    )(x, shift, scale, w1, wcs)


def _qkv_body(kvin_ref, cq_ref, gq_ref, gkv_ref, wq1_ref, wq2_ref, wkn_ref, wkr_ref, wv_ref,
              qc_ref, qs_ref, kcs_ref, q_ref, k_ref, v_ref):
    nq = _rms(cq_ref[...], gq_ref[...]).astype(BF16)
    q1 = _dot(nq, wq1_ref[...])
    q2 = _dot(nq, wq2_ref[...])
    kvin = kvin_ref[...]
    n = _rms(kvin[:, :KV_LORA], gkv_ref[...]).astype(BF16)
    kr = (kvin[:, KV_LORA:] * kcs_ref[...]).astype(BF16)
    kk = _dot(n, wkn_ref[...]) + _dot(kr, wkr_ref[...])
    v = _dot(n, wv_ref[...])
    qc = qc_ref[...]
    qs = qs_ref[...]
    scale = (QK_NOPE + QK_ROPE) ** -0.5 * math.log2(math.e)
    for h in range(MLA_HEADS):
        sl = slice(h * HEAD_SLOT, (h + 1) * HEAD_SLOT)
        q_ref[0, h] = ((q1[:, sl] * qc + q2[:, sl] * qs) * scale).astype(BF16)
        k_ref[0, h] = kk[:, sl].astype(BF16)
    for p in range(MLA_HEADS // 2):
        v_ref[0, p] = v[:, p * V7X_LANES:(p + 1) * V7X_LANES].astype(BF16)


def qkv_proj(kvin, cq, gq, gkv, wq1, wq2, wkn, wkr, wv, qc, qs, kcs, B, S):
    tm = TOKEN_TILE
    tps = S // tm
    row = lambda i: (i, 0)
    tab = lambda i: (i % tps, 0)
    seq = lambda i: (i // tps, 0, i % tps, 0)
    return pl.pallas_call(
        _qkv_body,
        grid=(B * tps,),
        in_specs=[pl.BlockSpec((tm, KVIN_WIDTH), row), pl.BlockSpec((tm, Q_LORA), row),
                  _full_spec(gq), _full_spec(gkv), _full_spec(wq1), _full_spec(wq2),
                  _full_spec(wkn), _full_spec(wkr), _full_spec(wv),
                  pl.BlockSpec((tm, HEAD_SLOT), tab), pl.BlockSpec((tm, HEAD_SLOT), tab),
                  pl.BlockSpec((tm, V7X_LANES), tab)],
        out_specs=[pl.BlockSpec((1, MLA_HEADS, tm, HEAD_SLOT), seq),
                   pl.BlockSpec((1, MLA_HEADS, tm, HEAD_SLOT), seq),
                   pl.BlockSpec((1, MLA_HEADS // 2, tm, V7X_LANES), seq)],
        out_shape=[jax.ShapeDtypeStruct((B, MLA_HEADS, S, HEAD_SLOT), BF16),
                   jax.ShapeDtypeStruct((B, MLA_HEADS, S, HEAD_SLOT), BF16),
                   jax.ShapeDtypeStruct((B, MLA_HEADS // 2, S, V7X_LANES), BF16)],
        compiler_params=_params("arbitrary"),
        name="qkv_proj",
    )(kvin, cq, gq, gkv, wq1, wq2, wkn, wkr, wv, qc, qs, kcs)


def _attn_body(*refs, n_sets):
    q_ref = refs[0]
    k_refs = refs[1:1 + 2 * n_sets:2]
    v_refs = refs[2:2 + 2 * n_sets:2]
    o_ref = refs[-1]
    for hp in range(MLA_HEADS // 2):
        outs = []
        for hh in range(2):
            h = 2 * hp + hh
            q = q_ref[0, h]
            s = [lax.dot_general(q, k_ref[0, h], (((1,), (1,)), ((), ())), preferred_element_type=F32)
                 for k_ref in k_refs]
            m = functools.reduce(jnp.maximum, [jnp.max(si, axis=-1, keepdims=True) for si in s])
            p = [jnp.exp2(si - m) for si in s]
            l = functools.reduce(jnp.add, [jnp.sum(pi, axis=-1, keepdims=True) for pi in p])
            o = functools.reduce(jnp.add, [_dot(pi.astype(BF16), v_ref[0, hp]) for pi, v_ref in zip(p, v_refs)])
            outs.append(o / l)
        lane = lax.broadcasted_iota(jnp.int32, outs[0].shape, 1)
        o_ref[0, :, hp * V7X_LANES:(hp + 1) * V7X_LANES] = jnp.where(lane < V_DIM, outs[0], outs[1]).astype(BF16)


def attention(q, kv_sets):
    B, H, S, _ = q.shape
    tq = TOKEN_TILE
    in_specs = [pl.BlockSpec((1, H, tq, HEAD_SLOT), lambda b, i: (b, 0, i, 0))]
    args = [q]
    for k, v in kv_sets:
        sk = k.shape[2]
        in_specs.append(pl.BlockSpec((1, H, sk, HEAD_SLOT), lambda b, i: (b, 0, 0, 0)))
        in_specs.append(pl.BlockSpec((1, H // 2, sk, V7X_LANES), lambda b, i: (b, 0, 0, 0)))
        args += [k, v]
    return pl.pallas_call(
        functools.partial(_attn_body, n_sets=len(kv_sets)),
        grid=(B, S // tq),
        in_specs=in_specs,
        out_specs=pl.BlockSpec((1, tq, H * V_DIM), lambda b, i: (b, i, 0)),
        out_shape=jax.ShapeDtypeStruct((B, S, H * V_DIM), BF16),
        compiler_params=_params("arbitrary", "arbitrary"),
        name="attention",
    )(*args)


def _fill_window(win_ref, x_ref, j, n_chunks, tc, halo):
    c = win_ref.shape[1]
    start = pl.multiple_of(j * tc, tc)
    win_ref[halo:halo + tc, :] = x_ref[0, pl.ds(start, tc), :]
    lo = pl.multiple_of(jnp.maximum(j * tc - halo, 0), halo)
    prev = x_ref[0, pl.ds(lo, halo), :]
    win_ref[0:halo, :] = jnp.where(j > 0, prev, jnp.zeros((halo, c), F32))
    hi = pl.multiple_of(jnp.minimum((j + 1) * tc, (n_chunks - 1) * tc + tc - halo), halo)
    nxt = x_ref[0, pl.ds(hi, halo), :]
    win_ref[halo + tc:2 * halo + tc, :] = jnp.where(j < n_chunks - 1, nxt, jnp.zeros((halo, c), F32))


def _conv_taps(win_ref, w_ref, b_ref, out_ref, tc, first_row, n_taps, row_block=64):
    c = win_ref.shape[1]
    for r0 in range(0, tc, row_block):
        for l0 in range(0, c, V7X_LANES):
            ls = slice(l0, l0 + V7X_LANES)
            acc = jnp.broadcast_to(b_ref[:, ls], (row_block, V7X_LANES))
            for k in range(n_taps):
                r = first_row + r0 + k
                acc = acc + win_ref[r:r + row_block, ls] * w_ref[k:k + 1, ls]
            out_ref[r0:r0 + row_block, ls] = acc


CONF_HALO = 16
CONF_ROW_BLOCK = 64


def _conf_body(u_ref, w_ref, b_ref, g_ref, beta_ref, o_ref, win_ref, sh_ref, acc_ref, *, n_chunks, tc):
    j = pl.program_id(1)
    _fill_window(win_ref, u_ref, j, n_chunks, tc, CONF_HALO)
    span = sh_ref.shape[1]
    for s in range(V7X_SUBLANES):
        sh_ref[s] = win_ref[s:s + span, :]
    first_row = CONF_HALO - (CONV_K - 1) // 2
    c = win_ref.shape[1]
    for r0 in range(0, tc, CONF_ROW_BLOCK):
        for l0 in range(0, c, V7X_LANES):
            ls = slice(l0, l0 + V7X_LANES)
            acc = jnp.broadcast_to(b_ref[:, ls], (CONF_ROW_BLOCK, V7X_LANES))
            for k in range(CONV_K):
                phase = (first_row + k) % V7X_SUBLANES
                base = r0 + (first_row + k) // V7X_SUBLANES * V7X_SUBLANES
                acc = acc + sh_ref[phase, base:base + CONF_ROW_BLOCK, ls] * w_ref[k:k + 1, ls]
            acc_ref[r0:r0 + CONF_ROW_BLOCK, ls] = acc
    y = _ln(acc_ref[...]) * g_ref[...] + beta_ref[...]
    o_ref[0] = (y * _sigmoid(y)).astype(BF16)


def conformer(u, w, b, g, beta):
    B, S, C = u.shape
    tc = TOKEN_TILE
    n_chunks = S // tc
    span = tc + 2 * CONF_HALO - V7X_SUBLANES
    return pl.pallas_call(
        functools.partial(_conf_body, n_chunks=n_chunks, tc=tc),
        grid=(B, n_chunks),
        in_specs=[pl.BlockSpec((1, S, C), lambda bb, j: (bb, 0, 0)),
                  _full_spec(w), _full_spec(b), _full_spec(g), _full_spec(beta)],
        out_specs=pl.BlockSpec((1, tc, C), lambda bb, j: (bb, j, 0)),
        out_shape=jax.ShapeDtypeStruct((B, S, C), BF16),
        scratch_shapes=[pltpu.VMEM((tc + 2 * CONF_HALO, C), F32), pltpu.VMEM((V7X_SUBLANES, span, C), F32),
                        pltpu.VMEM((tc, C), F32)],
        compiler_params=_params("arbitrary", "arbitrary"),
        name="conformer",
    )(u, w, b, g, beta)


def _fnet_body(c_ref, s_ref, a_ref, b_ref, o_ref, *, norm):
    y = _dot(c_ref[...], a_ref[0]) + _dot(s_ref[...], b_ref[0])
    o_ref[0] = (y * norm).astype(BF16)


def fnet_seq_dft(cmat, smat, fa, fb):
    B, S, W = fa.shape
    ts = min(S, 512)
    norm = 1.0 / math.sqrt(S * (FNET_WIDTH // FNET_GROUPS))
    return pl.pallas_call(
        functools.partial(_fnet_body, norm=norm),
        grid=(S // ts, B),
        in_specs=[pl.BlockSpec((ts, S), lambda i, b: (i, 0)), pl.BlockSpec((ts, S), lambda i, b: (i, 0)),
                  pl.BlockSpec((1, S, W), lambda i, b: (b, 0, 0)), pl.BlockSpec((1, S, W), lambda i, b: (b, 0, 0))],
        out_specs=pl.BlockSpec((1, ts, W), lambda i, b: (b, i, 0)),
        out_shape=jax.ShapeDtypeStruct((B, S, W), BF16),
        compiler_params=_params("arbitrary", "arbitrary"),
        name="fnet_seq_dft",
    )(cmat, smat, fa, fb)


LRU_HALO = 8
LRU_GROUP = V7X_SUBLANES
LRU_UNROLL = 8


def _lru_body(*refs, n_chunks, tc, reverse):
    if reverse:
        (x_ref, cw_ref, cb_ref, wg_ref, bg_ref, ca_ref, h0_ref, hf_ref, gb_ref,
         o_ref, s_ref, win_ref, xc_ref, a_ref, u_ref, carry_ref) = refs
    else:
        (x_ref, cw_ref, cb_ref, wg_ref, bg_ref, ca_ref, h0_ref,
         o_ref, s_ref, win_ref, xc_ref, a_ref, u_ref, carry_ref) = refs
    step = pl.program_id(1)
    j = (n_chunks - 1 - step) if reverse else step

    @pl.when(step == 0)
    def _():
        carry_ref[...] = jnp.broadcast_to(h0_ref[0], carry_ref.shape)

    _fill_window(win_ref, x_ref, j, n_chunks, tc, LRU_HALO)
    _conv_taps(win_ref, cw_ref, cb_ref, xc_ref, tc, LRU_HALO - LRU_CONV_K // 2, LRU_CONV_K)
    xc = xc_ref[...]
    gates = _dot(xc.astype(BF16), wg_ref[...]) + bg_ref[...]
    r = _sigmoid(gates[:, :LRU_WIDTH])
    i = _sigmoid(gates[:, LRU_WIDTH:])
    log_a = ca_ref[...] * r
    a = jnp.exp(log_a)
    a_ref[...] = a
    u_ref[...] = jnp.sqrt(-jnp.tanh(log_a) * (a * a + 1.0)) * (i * xc)

    n_groups = tc // LRU_GROUP
    row = lax.broadcasted_iota(jnp.int32, (LRU_GROUP, LRU_WIDTH), 0)

    def group(gi, carry):
        g = (n_groups - 1 - gi) if reverse else gi
        rows = pl.ds(pl.multiple_of(g * LRU_GROUP, LRU_GROUP), LRU_GROUP)
        a = a_ref[rows, :]
        u = u_ref[rows, :]
        for sh in (1, 2, 4):
            if reverse:
                a_sh = pltpu.roll(a, LRU_GROUP - sh, 0)
                u_sh = pltpu.roll(u, LRU_GROUP - sh, 0)
                valid = row < LRU_GROUP - sh
            else:
                a_sh = pltpu.roll(a, sh, 0)
                u_sh = pltpu.roll(u, sh, 0)
                valid = row >= sh
            a_sh = jnp.where(valid, a_sh, 1.0)
            u_sh = jnp.where(valid, u_sh, 0.0)
            u = a * u_sh + u
            a = a * a_sh
        h = u + a * carry
        u_ref[rows, :] = h
        last = h[0:1] if reverse else h[LRU_GROUP - 1:LRU_GROUP]
        return jnp.broadcast_to(last, h.shape)

    carry = lax.fori_loop(0, n_groups, group, carry_ref[...], unroll=LRU_UNROLL)
    carry_ref[...] = carry
    h = u_ref[...]
    if reverse:
        gb = gb_ref[0]
        gelu = 0.5 * gb * (1.0 + jnp.tanh(math.sqrt(2.0 / math.pi) * (gb + 0.044715 * (gb * gb * gb))))
        o_ref[0] = ((hf_ref[0] + h) * gelu).astype(BF16)
    else:
        o_ref[0] = h

    @pl.when(step == n_chunks - 1)
    def _():
        s_ref[0] = carry[0:1]


def lru_scan(xb, cw, cb, wg, bg, ca, h0, hf=None, gb=None):
    B, S, W = xb.shape
    reverse = hf is not None
    tc = TOKEN_TILE
    n_chunks = S // tc
    chunk = (lambda bb, s: (bb, n_chunks - 1 - s, 0)) if reverse else (lambda bb, s: (bb, s, 0))
    per_b = lambda bb, s: (bb, 0, 0)
    in_specs = [pl.BlockSpec((1, S, W), per_b), _full_spec(cw), _full_spec(cb), _full_spec(wg),
                _full_spec(bg), _full_spec(ca), pl.BlockSpec((1, 1, W), per_b)]
    args = [xb, cw, cb, wg, bg, ca, h0]
    if reverse:
        in_specs += [pl.BlockSpec((1, tc, W), chunk), pl.BlockSpec((1, tc, W), chunk)]
        args += [hf, gb]
    return pl.pallas_call(
        functools.partial(_lru_body, n_chunks=n_chunks, tc=tc, reverse=reverse),
        grid=(B, n_chunks),
        in_specs=in_specs,
        out_specs=[pl.BlockSpec((1, tc, W), chunk), pl.BlockSpec((1, 1, W), per_b)],
        out_shape=[jax.ShapeDtypeStruct((B, S, W), BF16 if reverse else F32),
                   jax.ShapeDtypeStruct((B, 1, W), F32)],
        scratch_shapes=[pltpu.VMEM((tc + 2 * LRU_HALO, W), F32), pltpu.VMEM((tc, W), F32),
                        pltpu.VMEM((tc, W), F32), pltpu.VMEM((tc, W), F32),
                        pltpu.VMEM((LRU_GROUP, W), F32)],
        compiler_params=_params("arbitrary", "arbitrary"),
        name="lru_bwd" if reverse else "lru_fwd",
    )(*args)


def _split_bf16(x):
    hi = x.astype(BF16)
    return hi, (x - hi.astype(F32)).astype(BF16)


def _merge_body(x_ref, sh1_ref, sc1_ref, g1_ref, sh2_ref, sc2_ref, attn_ref, conf_ref, fnet_ref, lru_ref,
                wgl_ref, wb_ref, wout_ref, lng_ref, lnb_ref, wrh_ref, wrl_ref,
                y_ref, h2_ref, aff_ref):
    x = x_ref[...]
    h = (_ln(x) * (1.0 + sc1_ref[0]) + sh1_ref[0]).astype(BF16)
    acc = None
    for k, br_ref in enumerate((attn_ref, conf_ref, fnet_ref, lru_ref)):
        gl = _dot(h, wgl_ref[:, k * D_MODEL:(k + 1) * D_MODEL])
        term = _sigmoid(gl) * _dot(br_ref[...], wb_ref[k])
        acc = term if acc is None else acc + term
    out = _dot(acc.astype(BF16), wout_ref[...])
    y = _ln(DEEPNORM_ALPHA * x + g1_ref[0] * out) * lng_ref[...] + lnb_ref[...]
    y_ref[...] = y
    h2 = _ln(y) * (1.0 + sc2_ref[0]) + sh2_ref[0]
    h2_ref[...] = h2.astype(BF16)
    hh, hl = _split_bf16(h2)
    logits = _dot(hh, wrh_ref[...]) + _dot(hh, wrl_ref[...]) + _dot(hl, wrh_ref[...])
    lane = lax.broadcasted_iota(jnp.int32, logits.shape, 1)
    valid = lane < N_EXPERTS
    logits = jnp.where(valid, logits, -jnp.inf)
    e = jnp.exp(logits - jnp.max(logits, axis=-1, keepdims=True))
    aff_ref[...] = e / jnp.sum(e, axis=-1, keepdims=True)


def merge(x, mods, branches, wgl, wb, wout, lng, lnb, wrh, wrl, tiles_per_mod):
    M = x.shape[0]
    tm = TOKEN_TILE
    row = lambda i: (i, 0)
    sh1, sc1, g1, sh2, sc2 = mods
    in_specs = ([pl.BlockSpec((tm, D_MODEL), row)] + [_mod_spec(m, tiles_per_mod) for m in mods]
                + [pl.BlockSpec((tm, BRANCH_WIDTH), row)] * N_BRANCH
                + [_full_spec(a) for a in (wgl, wb, wout, lng, lnb, wrh, wrl)])
    return pl.pallas_call(
        _merge_body,
        grid=(M // tm,),
        in_specs=in_specs,
        out_specs=[pl.BlockSpec((tm, D_MODEL), row), pl.BlockSpec((tm, D_MODEL), row),
                   pl.BlockSpec((tm, V7X_LANES), row)],
        out_shape=[jax.ShapeDtypeStruct((M, D_MODEL), F32), jax.ShapeDtypeStruct((M, D_MODEL), BF16),
                   jax.ShapeDtypeStruct((M, V7X_LANES), F32)],
        compiler_params=_params("arbitrary"),
        name="merge",
    )(x, sh1, sc1, g1, sh2, sc2, *branches, wgl, wb, wout, lng, lnb, wrh, wrl)


def _select_body(aff_ref, tri_ref, pos_ref, rng_ref, *, cap, n_tok, blk):
    bits = pltpu.bitcast(aff_ref[0], jnp.int32)

    def bisect(it, lo):
        cand = lo | jnp.left_shift(jnp.int32(1), 30 - it)
        cnt = jnp.sum(jnp.where(bits >= cand, 1.0, 0.0), axis=1, keepdims=True)
        return jnp.where(cnt >= cap, cand, lo)

    thr = lax.fori_loop(0, 31, bisect, jnp.zeros((N_EXPERTS, 1), jnp.int32))
    need = cap - jnp.sum(jnp.where(bits > thr, 1.0, 0.0), axis=1, keepdims=True)
    tri = tri_ref[...]
    ties_before = jnp.zeros((N_EXPERTS, 1), F32)
    sel_before = jnp.zeros((N_EXPERTS, 1), F32)
    lane = lax.broadcasted_iota(jnp.int32, (N_EXPERTS, V7X_LANES), 1)
    first_slot = jnp.zeros((N_EXPERTS, V7X_LANES), F32)
    for c0 in range(0, n_tok, V7X_LANES):
        sl = slice(c0, c0 + V7X_LANES)
        if c0 % blk == 0:
            first_slot = jnp.where(lane == c0 // blk, sel_before, first_slot)
        bits_c = bits[:, sl]
        gt_c = jnp.where(bits_c > thr, 1.0, 0.0)
        eq_c = jnp.where(bits_c == thr, 1.0, 0.0)
        tie_rank = _dot(eq_c.astype(BF16), tri) + ties_before
        ties_before = ties_before + jnp.sum(eq_c, axis=1, keepdims=True)
        sel_c = gt_c + eq_c * jnp.where(tie_rank < need, 1.0, 0.0)
        slot = _dot(sel_c.astype(BF16), tri) + sel_before
        sel_before = sel_before + jnp.sum(sel_c, axis=1, keepdims=True)
        pos_ref[0, :, sl] = jnp.where(sel_c > 0.0, slot, -1.0)

    rng_ref[0] = first_slot


def select_tokens(aff_t, tri, cap, blk):
    B, E, N = aff_t.shape
    return pl.pallas_call(
        functools.partial(_select_body, cap=cap, n_tok=N, blk=blk),
        grid=(B,),
        in_specs=[pl.BlockSpec((1, E, N), lambda b: (b, 0, 0)), _full_spec(tri)],
        out_specs=[pl.BlockSpec((1, E, N), lambda b: (b, 0, 0)),
                   pl.BlockSpec((1, E, V7X_LANES), lambda b: (b, 0, 0))],
        out_shape=[jax.ShapeDtypeStruct((B, E, N), F32), jax.ShapeDtypeStruct((B, E, V7X_LANES), F32)],
        compiler_params=_params("arbitrary"),
        name="select_tokens",
    )(aff_t, tri)


def _experts_body(first_ref, pos_ref, h_ref, wg_ref, wu_ref, wd_ref, y_ref, xe_ref, *, cap, blk, n_blk, win):
    e = pl.program_id(0)
    b = pl.program_id(1)
    base = (b * N_EXPERTS + e) * n_blk
    xe_ref[...] = jnp.zeros(xe_ref.shape, F32)
    for t in range(n_blk):
        first = first_ref[base + t]
        s0 = pl.multiple_of(lax.shift_left(lax.shift_right_logical(first, EXPERT_ALIGN_LOG2), EXPERT_ALIGN_LOG2),
                            1 << EXPERT_ALIGN_LOG2)
        slot = lax.broadcasted_iota(jnp.int32, (win, 1), 0) + s0
        pos = pos_ref[0, t].astype(jnp.int32)
        onehot = jnp.where(slot == pos, 1.0, 0.0).astype(BF16)
        xe_ref[pl.ds(s0, win), :] += _dot(onehot, h_ref[t * blk:(t + 1) * blk, :])
    xe = xe_ref[0:cap, :].astype(BF16)
    g = _dot(xe, wg_ref[0])
    u = _dot(xe, wu_ref[0])
    he = (g * _sigmoid(g) * u).astype(BF16)
    y_ref[0, 0] = _dot(he, wd_ref[0]).astype(BF16)


def experts(first_slot, pos_blocks, h2, wg, wu, wd, B, N, cap, blk):
    E = N_EXPERTS
    n_blk = N // blk
    win = min(blk, cap) + (1 << EXPERT_ALIGN_LOG2)
    grid_spec = pltpu.PrefetchScalarGridSpec(
        num_scalar_prefetch=1,
        grid=(E, B),
        in_specs=[pl.BlockSpec((1, n_blk, 1, blk), lambda e, b, r: (b * E + e, 0, 0, 0)),
                  pl.BlockSpec((N, D_MODEL), lambda e, b, r: (b, 0)),
                  pl.BlockSpec((1, D_MODEL, EXPERT_FF), lambda e, b, r: (e, 0, 0)),
                  pl.BlockSpec((1, D_MODEL, EXPERT_FF), lambda e, b, r: (e, 0, 0)),
                  pl.BlockSpec((1, EXPERT_FF, D_MODEL), lambda e, b, r: (e, 0, 0))],
        out_specs=pl.BlockSpec((1, 1, cap, D_MODEL), lambda e, b, r: (b, e, 0, 0)),
        scratch_shapes=[pltpu.VMEM((cap + win, D_MODEL), F32)])
    return pl.pallas_call(
        functools.partial(_experts_body, cap=cap, blk=blk, n_blk=n_blk, win=win),
        grid_spec=grid_spec,
        out_shape=jax.ShapeDtypeStruct((B, E, cap, D_MODEL), BF16),
        compiler_params=_params("arbitrary", "arbitrary"),
        name="experts",
    )(first_slot, pos_blocks, h2, wg, wu, wd)


def _combine_body(first_ref, x_ref, g2_ref, posc_ref, aff_ref, ye_ref, lng_ref, lnb_ref, o_ref,
                  *, cap, sub, win, tiles_per_batch, n_sub_batch):
    i = pl.program_id(0)
    b = i // tiles_per_batch
    sub_tiles = x_ref.shape[0] // sub
    first_sub = (i % tiles_per_batch) * sub_tiles
    lane_slot = lax.broadcasted_iota(jnp.int32, (1, win), 1)
    for st in range(sub_tiles):
        rows = slice(st * sub, (st + 1) * sub)
        posc = posc_ref[0, rows, :]
        aff = aff_ref[rows, :]
        acc = None
        for e in range(N_EXPERTS):
            first = first_ref[(b * N_EXPERTS + e) * n_sub_batch + first_sub + st]
            s0 = lax.shift_left(lax.shift_right_logical(first, EXPERT_ALIGN_LOG2), EXPERT_ALIGN_LOG2)
            s0 = pl.multiple_of(jnp.minimum(s0, cap - win), 1 << EXPERT_ALIGN_LOG2)
            pcol = posc[:, e:e + 1]
            onehot = jnp.where(pcol == (lane_slot + s0).astype(F32), 1.0, 0.0).astype(BF16)
            w = jnp.where(pcol >= 0.0, aff[:, e:e + 1], 0.0)
            term = w * _dot(onehot, ye_ref[0, e, pl.ds(s0, win), :])
            acc = term if acc is None else acc + term
        y = _ln(DEEPNORM_ALPHA * x_ref[rows, :] + g2_ref[0] * acc)
        o_ref[rows, :] = y * lng_ref[...] + lnb_ref[...]


def combine(first_slot, x, gate2, posc, aff, ye, lng, lnb, B, N, cap, sub):
    tm = min(N, 512)
    tpb = N // tm
    win = min(sub + (1 << EXPERT_ALIGN_LOG2), cap)
    row = lambda i, f: (i, 0)
    gate_spec = (pl.BlockSpec((1, 1, D_MODEL), lambda i, f: (0, 0, 0)) if gate2.shape[0] == 1
                 else pl.BlockSpec((1, 1, D_MODEL), lambda i, f: (i // tpb, 0, 0)))
    grid_spec = pltpu.PrefetchScalarGridSpec(
        num_scalar_prefetch=1,
        grid=(B * tpb,),
        in_specs=[pl.BlockSpec((tm, D_MODEL), row), gate_spec,
                  pl.BlockSpec((1, tm, V7X_LANES), lambda i, f: (i // tpb, i % tpb, 0)),
                  pl.BlockSpec((tm, V7X_LANES), row),
                  pl.BlockSpec((1, N_EXPERTS, cap, D_MODEL), lambda i, f: (i // tpb, 0, 0, 0)),
                  pl.BlockSpec(lng.shape, lambda i, f: (0, 0)), pl.BlockSpec(lnb.shape, lambda i, f: (0, 0))],
        out_specs=pl.BlockSpec((tm, D_MODEL), row))
    return pl.pallas_call(
        functools.partial(_combine_body, cap=cap, sub=sub, win=win, tiles_per_batch=tpb, n_sub_batch=N // sub),
        grid_spec=grid_spec,
        out_shape=jax.ShapeDtypeStruct((B * N, D_MODEL), F32),
        compiler_params=_params("arbitrary"),
        name="combine",
    )(first_slot, x, gate2, posc, aff, ye, lng, lnb)


def _rope_rotation():
    r = [[0.0] * QK_ROPE for _ in range(QK_ROPE)]
    half = QK_ROPE // 4
    for base in (0, QK_ROPE // 2):
        for j in range(half):
            r[base + j + half][base + j] = -1.0
            r[base + j][base + j + half] = 1.0
    return jnp.array(r, F32)


def _rope_tables(rows):
    row = jnp.repeat(jnp.arange(rows), GRID_W).astype(F32)
    col = jnp.tile(jnp.arange(GRID_W), rows).astype(F32)
    inv = ROPE_BASE ** (-jnp.arange(ROPE_AXIS_FREQ, dtype=F32) / ROPE_AXIS_FREQ)
    ang_r = row[:, None] * inv
    ang_c = col[:, None] * inv
    cos = jnp.concatenate([jnp.cos(ang_r)] * 2 + [jnp.cos(ang_c)] * 2, axis=-1)
    sin = jnp.concatenate([jnp.sin(ang_r)] * 2 + [jnp.sin(ang_c)] * 2, axis=-1)
    return cos, sin


def _position_tables(cos, sin):
    S = cos.shape[0]
    ones = jnp.ones((S, QK_NOPE), F32)
    pad = jnp.zeros((S, HEAD_SLOT - QK_NOPE - QK_ROPE), F32)
    qc = jnp.concatenate([ones, cos, pad], axis=-1)
    qs = jnp.concatenate([jnp.zeros((S, QK_NOPE), F32), sin, pad], axis=-1)
    kcs = jnp.concatenate([cos, sin, jnp.zeros((S, V7X_LANES - 2 * QK_ROPE), F32)], axis=-1)
    return qc, qs, kcs


def _dft_matrices(S):
    Q = 1 << (S.bit_length() // 2)
    P = S // Q
    j = jnp.arange(S, dtype=jnp.int32)[:, None]

    def table(step, n):
        ang = ((j * (step * jnp.arange(n, dtype=jnp.int32))[None, :]) % S).astype(F32) * (2.0 * math.pi / S)
        return jnp.cos(ang), jnp.sin(ang)

    c1, s1 = table(Q, P)
    c2, s2 = table(1, Q)
    cos = c1[:, :, None] * c2[:, None, :] - s1[:, :, None] * s2[:, None, :]
    sin = s1[:, :, None] * c2[:, None, :] + c1[:, :, None] * s2[:, None, :]
    return cos.reshape(S, S).astype(BF16), (-sin).reshape(S, S).astype(BF16)


def _channel_dft_weights():
    n = FNET_WIDTH // FNET_GROUPS
    k = jnp.arange(n, dtype=jnp.int32)
    ang = ((k[:, None] * k[None, :]) % n).astype(F32) * (2.0 * math.pi / n)
    eye = jnp.eye(FNET_GROUPS, dtype=F32)
    return jnp.concatenate([jnp.kron(eye, jnp.cos(ang)), jnp.kron(eye, jnp.sin(ang))], axis=1).astype(BF16)


def _block_diag_heads(w):
    H, d, _ = w.shape
    out = jnp.zeros((H, d, H, d), w.dtype)
    out = out.at[jnp.arange(H), :, jnp.arange(H), :].set(w)
    return out.reshape(H * d, H * d)


def _layer_weights(l, w_in, q_norm_g, w_uq, kv_norm_g, w_ukv, lru_wa, lru_ba, lru_wx, lru_bx, lru_lambda,
                   w_branch, w_out, w_router, rot):
    wi = w_in[l]
    o = 0
    ckv = wi[:, o:o + KV_LORA]
    o += KV_LORA
    kr = wi[:, o:o + QK_ROPE]
    o += QK_ROPE
    xb = wi[:, o:o + LRU_WIDTH]
    o += LRU_WIDTH
    cq = wi[:, o:o + Q_LORA]
    o += Q_LORA
    glu = wi[:, o:o + 2 * CONV_WIDTH]
    o += 2 * CONV_WIDTH
    f = wi[:, o:o + FNET_WIDTH]
    o += FNET_WIDTH
    gb = wi[:, o:o + LRU_WIDTH]
    o += LRU_WIDTH
    gl = wi[:, o:]
    kpad = jnp.zeros((D_MODEL, V7X_LANES - 2 * QK_ROPE), F32)
    w1 = jnp.concatenate([ckv, kr, kr @ rot, kpad, xb, cq, glu, f, gb], axis=1).astype(BF16)

    uq = w_uq[l]
    zq = jnp.zeros((Q_LORA, MLA_HEADS, HEAD_SLOT - QK_NOPE - QK_ROPE), F32)
    wq1 = jnp.concatenate([uq, zq], axis=-1).reshape(Q_LORA, MLA_HEADS * HEAD_SLOT).astype(BF16)
    uq_rot = jnp.einsum('rhd,de->rhe', uq[..., QK_NOPE:], rot)
    wq2 = jnp.concatenate([jnp.zeros((Q_LORA, MLA_HEADS, QK_NOPE), F32), uq_rot, zq], axis=-1)
    wq2 = wq2.reshape(Q_LORA, MLA_HEADS * HEAD_SLOT).astype(BF16)
    ukv = w_ukv[l]
    zk = jnp.zeros((KV_LORA, MLA_HEADS, HEAD_SLOT - QK_NOPE), F32)
    wkn = jnp.concatenate([ukv[..., :QK_NOPE], zk], axis=-1).reshape(KV_LORA, MLA_HEADS * HEAD_SLOT).astype(BF16)
    wv = ukv[..., QK_NOPE:].reshape(KV_LORA, MLA_HEADS * V_DIM).astype(BF16)
    eye = jnp.eye(QK_ROPE, dtype=F32)
    slot = jnp.concatenate([jnp.zeros((QK_ROPE, QK_NOPE), F32), eye,
                            jnp.zeros((QK_ROPE, HEAD_SLOT - QK_NOPE - QK_ROPE), F32)], axis=1)
    slot = jnp.tile(slot, (1, MLA_HEADS))
    wkr = jnp.concatenate([slot, slot, jnp.zeros((V7X_LANES - 2 * QK_ROPE, MLA_HEADS * HEAD_SLOT), F32)],
                          axis=0).astype(BF16)

    lru = []
    for d in range(2):
        wg = jnp.concatenate([_block_diag_heads(lru_wa[l, d]), _block_diag_heads(lru_wx[l, d])], axis=1).astype(BF16)
        bg = jnp.concatenate([lru_ba[l, d], lru_bx[l, d]])[None, :]
        ca = (-LRU_C * jax.nn.softplus(-lru_lambda[l, d]))[None, :]
        lru.append((wg, bg, ca))

    wr = jnp.pad(w_router[l], ((0, 0), (0, V7X_LANES - N_EXPERTS)))
    wrh, wrl = _split_bf16(wr)
    return dict(w1=w1, wq1=wq1, wq2=wq2, wkn=wkn, wkr=wkr, wv=wv, gq=q_norm_g[l][None, :],
                gkv=kv_norm_g[l][None, :], lru=lru, wgl=gl.astype(BF16), wb=w_branch[l].astype(BF16),
                wout=w_out[l].astype(BF16), wrh=wrh, wrl=wrl)


def _sequence_mixers(B, S, proj, lw, cv, lru_conv, tabs, dft, h0, need_branches=True):
    kvin, xb, cq, u, fa, fb, gb = proj
    q, k, v = qkv_proj(kvin, cq, lw['gq'], lw['gkv'], lw['wq1'], lw['wq2'], lw['wkn'], lw['wkr'], lw['wv'],
                       *tabs, B, S)
    conf = fnet = None
    if need_branches:
        conf = conformer(u.reshape(B, S, CONV_WIDTH), *cv)
        fnet = fnet_seq_dft(*dft, fa.reshape(B, S, FNET_WIDTH), fb.reshape(B, S, FNET_WIDTH))
    xb3 = xb.reshape(B, S, LRU_WIDTH)
    hf, s_fwd = lru_scan(xb3, *lru_conv, *lw['lru'][0], h0[0])
    hrec, s_bwd = lru_scan(xb3, *lru_conv, *lw['lru'][1], h0[1], hf=hf, gb=gb.reshape(B, S, LRU_WIDTH))
    return q, k, v, conf, fnet, hrec, (s_fwd, s_bwd)


def _moe(y, h2, aff, gate2, lw_e, ln2, tri, B, N):
    cap = EC_CAPACITY * N // N_EXPERTS
    aff_t = jnp.swapaxes(aff.reshape(B, N, V7X_LANES)[:, :, :N_EXPERTS], 1, 2)
    blk, sub = EXPERT_TOKEN_BLOCK, COMBINE_TOKEN_BLOCK
    pos, first_slot = select_tokens(aff_t, tri, cap, sub)
    first_slot = first_slot[:, :, :N // sub].astype(jnp.int32)
    first_blk = first_slot[:, :, ::blk // sub].reshape(-1)
    ye = experts(first_blk, pos.reshape(B * N_EXPERTS, N // blk, 1, blk), h2, *lw_e, B, N, cap, blk)
    posc = jnp.pad(jnp.swapaxes(pos, 1, 2), ((0, 0), (0, 0), (0, V7X_LANES - N_EXPERTS)), constant_values=-1.0)
    return combine(first_slot.reshape(-1), y, gate2, posc, aff, ye, *ln2, B, N, cap, sub)


def kernel(x, c, ctx, c_ctx, ada_w, ada_b, w_in, q_norm_g, w_uq, kv_norm_g, w_ukv, cv_w, cv_b, cv_ln_g, cv_ln_b, lru_conv_w, lru_conv_b, lru_wa, lru_ba, lru_wx, lru_bx, lru_lambda, w_branch, w_out, ln1_g, ln1_b, w_router, w_e_gate, w_e_up, w_e_down, ln2_g, ln2_b):
    B, S, _ = x.shape
    SC = ctx.shape[1]
    rot = _rope_rotation()
    cos, sin = _rope_tables(S // GRID_W)
    tabs_l = _position_tables(cos, sin)
    tabs_c = _position_tables(jnp.ones((SC, QK_ROPE), F32), jnp.zeros((SC, QK_ROPE), F32))
    dft_l = _dft_matrices(S)
    dft_c = _dft_matrices(SC)
    wcs = _channel_dft_weights()
    tri = jnp.triu(jnp.ones((V7X_LANES, V7X_LANES), F32), k=1).astype(BF16)
    zero_state = jnp.zeros((B, 1, LRU_WIDTH), F32)

    x_lat = x.reshape(B * S, D_MODEL)
    x_ctx = ctx.reshape(B * SC, D_MODEL)
    tpm_l = S // TOKEN_TILE
    tpm_c = SC // TOKEN_TILE
    for l in range(DEPTH):
        ctx_out = l < DEPTH - 1
        lw = _layer_weights(l, w_in, q_norm_g, w_uq, kv_norm_g, w_ukv, lru_wa, lru_ba, lru_wx, lru_bx,
                            lru_lambda, w_branch, w_out, w_router, rot)
        cv = (cv_w[l], cv_b[l][None, :], cv_ln_g[l][None, :], cv_ln_b[l][None, :])
        lru_conv = (lru_conv_w[l], lru_conv_b[l][None, :])
        ln1 = (ln1_g[l][None, :], ln1_b[l][None, :])
        ln2 = (ln2_g[l][None, :], ln2_b[l][None, :])
        lw_e = (w_e_gate[l].astype(BF16), w_e_up[l].astype(BF16), w_e_down[l].astype(BF16))

        mod = jax.nn.silu(c) @ ada_w[l] + ada_b[l]
        mods_l = [m[:, None, :] for m in jnp.split(mod, 6, axis=-1)]
        mod_c = jax.nn.silu(c_ctx) @ ada_w[l] + ada_b[l]
        mods_c = [m[None, None, :] for m in jnp.split(mod_c, 6)]

        proj_c = in_proj(x_ctx, mods_c[0], mods_c[1], lw['w1'], wcs, SC)
        q_c, k_c, v_c, conf_c, fnet_c, hrec_c, states = _sequence_mixers(
            B, SC, proj_c, lw, cv, lru_conv, tabs_c, dft_c, (zero_state, zero_state), need_branches=ctx_out)
        proj_l = in_proj(x_lat, mods_l[0], mods_l[1], lw['w1'], wcs, S)
        q_l, k_l, v_l, conf_l, fnet_l, hrec_l, _ = _sequence_mixers(
            B, S, proj_l, lw, cv, lru_conv, tabs_l, dft_l, states)

        attn_l = attention(q_l, [(k_l, v_l), (k_c, v_c)])
        merge_w = (lw['wgl'], lw['wb'], lw['wout'], *ln1, lw['wrh'], lw['wrl'])
        sel = lambda ms: (ms[0], ms[1], ms[2], ms[3], ms[4])
        y_l, h2_l, aff_l = merge(x_lat, sel(mods_l),
                                 [attn_l.reshape(B * S, -1), conf_l.reshape(B * S, -1),
                                  fnet_l.reshape(B * S, -1), hrec_l.reshape(B * S, -1)], *merge_w, tpm_l)
        x_lat = _moe(y_l, h2_l, aff_l, mods_l[5], lw_e, ln2, tri, B, S)
        if ctx_out:
            attn_c = attention(q_c, [(k_c, v_c)])
            y_c, h2_c, aff_c = merge(x_ctx, sel(mods_c),
                                     [attn_c.reshape(B * SC, -1), conf_c.reshape(B * SC, -1),
                                      fnet_c.reshape(B * SC, -1), hrec_c.reshape(B * SC, -1)], *merge_w, tpm_c)
            x_ctx = _moe(y_c, h2_c, aff_c, mods_c[5], lw_e, ln2, tri, B, SC)
    return x_lat.reshape(B, S, D_MODEL)
```

```python
import functools
import math

import jax
import jax.numpy as jnp
from jax import lax
from jax.experimental import pallas as pl
from jax.experimental.pallas import tpu as pltpu

D_MODEL = 1024
DEPTH = 4
GRID_W = 64
MLA_HEADS = 8
Q_LORA = 384
KV_LORA = 256
QK_NOPE = 64
QK_ROPE = 32
V_DIM = 64
ROPE_AXIS_FREQ = QK_ROPE // 4
ROPE_BASE = 10000.0
CONV_WIDTH = 512
CONV_K = 31
FNET_WIDTH = 512
FNET_GROUPS = 4
LRU_WIDTH = 512
LRU_HEADS = 8
LRU_CONV_K = 4
LRU_C = 8.0
N_BRANCH = 4
BRANCH_WIDTH = 512
N_EXPERTS = 16
EXPERT_FF = 1408
EC_CAPACITY = 2
DEEPNORM_ALPHA = (2 * DEPTH) ** 0.25
LN_EPS = 1e-6

V7X_LANES = 128
V7X_SUBLANES = 8
V7X_VMEM_LIMIT_BYTES = 56 * 1024 * 1024
HEAD_SLOT = V7X_LANES
TOKEN_TILE = 256
IN_PROJ_TILE = 512
MERGE_TILE = 256
ATTN_Q_ROWS = 128
ATTN_KEY_CHUNK = 256
EXPERT_TOKEN_BLOCK = 256
COMBINE_TOKEN_BLOCK = 128
EXPERT_ALIGN_LOG2 = 4
KVIN_WIDTH = KV_LORA + V7X_LANES
IN_PROJ_WIDTH = KVIN_WIDTH + LRU_WIDTH + Q_LORA + 2 * CONV_WIDTH + FNET_WIDTH + LRU_WIDTH

F32 = jnp.float32
BF16 = jnp.bfloat16


def _params(*sem):
    return pltpu.CompilerParams(dimension_semantics=sem, vmem_limit_bytes=V7X_VMEM_LIMIT_BYTES)


def _sigmoid(x):
    return 1.0 / (1.0 + jnp.exp(-x))


def _ln(x):
    mu = jnp.mean(x, axis=-1, keepdims=True)
    xc = x - mu
    var = jnp.mean(xc * xc, axis=-1, keepdims=True)
    return xc * lax.rsqrt(var + LN_EPS)


def _rms(x, g):
    return x * lax.rsqrt(jnp.mean(x * x, axis=-1, keepdims=True) + LN_EPS) * g


def _dot(a, b):
    return jnp.dot(a, b, preferred_element_type=F32)


def _mod_spec(mods, tiles_per_mod):
    if mods.shape[0] == 1:
        return pl.BlockSpec((1, 1, D_MODEL), lambda i: (0, 0, 0))
    return pl.BlockSpec((1, 1, D_MODEL), lambda i: (i // tiles_per_mod, 0, 0))


def _full_spec(arr):
    zeros = (0,) * arr.ndim
    return pl.BlockSpec(arr.shape, lambda *_: zeros)


def _in_proj_body(x_ref, sh_ref, sc_ref, w_ref, wcs_ref,
                  kvin_ref, xb_ref, cq_ref, u_ref, fa_ref, fb_ref, gb_ref):
    h = _ln(x_ref[...]) * (1.0 + sc_ref[0]) + sh_ref[0]
    p = _dot(h.astype(BF16), w_ref[...])
    o = 0
    kvin_ref[...] = p[:, o:o + KVIN_WIDTH]
    o += KVIN_WIDTH
    xb_ref[...] = p[:, o:o + LRU_WIDTH]
    o += LRU_WIDTH
    cq_ref[...] = p[:, o:o + Q_LORA]
    o += Q_LORA
    a = p[:, o:o + CONV_WIDTH]
    g = p[:, o + CONV_WIDTH:o + 2 * CONV_WIDTH]
    u_ref[...] = a * _sigmoid(g)
    o += 2 * CONV_WIDTH
    f = p[:, o:o + FNET_WIDTH].astype(BF16)
    o += FNET_WIDTH
    gb_ref[...] = p[:, o:o + LRU_WIDTH]
    ab = _dot(f, wcs_ref[...])
    fa_ref[...] = ab[:, :FNET_WIDTH].astype(BF16)
    fb_ref[...] = ab[:, FNET_WIDTH:].astype(BF16)


def in_proj(x, shift, scale, w1, wcs, rows_per_mod):
    M = x.shape[0]
    tm = min(IN_PROJ_TILE, rows_per_mod)
    tiles_per_mod = rows_per_mod // tm
    row = lambda i: (i, 0)
    widths = (KVIN_WIDTH, LRU_WIDTH, Q_LORA, CONV_WIDTH, FNET_WIDTH, FNET_WIDTH, LRU_WIDTH)
    dtypes = (F32, F32, F32, F32, BF16, BF16, F32)
    return pl.pallas_call(
        _in_proj_body,
        grid=(M // tm,),
        in_specs=[pl.BlockSpec((tm, D_MODEL), row), _mod_spec(shift, tiles_per_mod),
                  _mod_spec(scale, tiles_per_mod), _full_spec(w1), _full_spec(wcs)],
        out_specs=[pl.BlockSpec((tm, w), row) for w in widths],
        out_shape=[jax.ShapeDtypeStruct((M, w), d) for w, d in zip(widths, dtypes)],
        compiler_params=_params("arbitrary"),
        name="in_proj",
    )(x, shift, scale, w1, wcs)


def _qkv_body(kvin_ref, cq_ref, gq_ref, gkv_ref, wq1_ref, wq2_ref, wkn_ref, wkr_ref, wv_ref,
              qc_ref, qs_ref, kcs_ref, q_ref, k_ref, v_ref):
    nq = _rms(cq_ref[...], gq_ref[...]).astype(BF16)
    q1 = _dot(nq, wq1_ref[...])
    q2 = _dot(nq, wq2_ref[...])
    kvin = kvin_ref[...]
    n = _rms(kvin[:, :KV_LORA], gkv_ref[...]).astype(BF16)
    kr = (kvin[:, KV_LORA:] * kcs_ref[...]).astype(BF16)
    kk = _dot(n, wkn_ref[...]) + _dot(kr, wkr_ref[...])
    v = _dot(n, wv_ref[...])
    qc = qc_ref[...]
    qs = qs_ref[...]
    scale = (QK_NOPE + QK_ROPE) ** -0.5 * math.log2(math.e)
    for h in range(MLA_HEADS):
        sl = slice(h * HEAD_SLOT, (h + 1) * HEAD_SLOT)
        q_ref[0, h] = ((q1[:, sl] * qc + q2[:, sl] * qs) * scale).astype(BF16)
        k_ref[0, h] = kk[:, sl].astype(BF16)
    for p in range(MLA_HEADS // 2):
        v_ref[0, p] = v[:, p * V7X_LANES:(p + 1) * V7X_LANES].astype(BF16)


def qkv_proj(kvin, cq, gq, gkv, wq1, wq2, wkn, wkr, wv, qc, qs, kcs, B, S):
    tm = TOKEN_TILE
    tps = S // tm
    row = lambda i: (i, 0)
    tab = lambda i: (i % tps, 0)
    seq = lambda i: (i // tps, 0, i % tps, 0)
    return pl.pallas_call(
        _qkv_body,
        grid=(B * tps,),
        in_specs=[pl.BlockSpec((tm, KVIN_WIDTH), row), pl.BlockSpec((tm, Q_LORA), row),
                  _full_spec(gq), _full_spec(gkv), _full_spec(wq1), _full_spec(wq2),
                  _full_spec(wkn), _full_spec(wkr), _full_spec(wv),
                  pl.BlockSpec((tm, HEAD_SLOT), tab), pl.BlockSpec((tm, HEAD_SLOT), tab),
                  pl.BlockSpec((tm, V7X_LANES), tab)],
        out_specs=[pl.BlockSpec((1, MLA_HEADS, tm, HEAD_SLOT), seq),
                   pl.BlockSpec((1, MLA_HEADS, tm, HEAD_SLOT), seq),
                   pl.BlockSpec((1, MLA_HEADS // 2, tm, V7X_LANES), seq)],
        out_shape=[jax.ShapeDtypeStruct((B, MLA_HEADS, S, HEAD_SLOT), BF16),
                   jax.ShapeDtypeStruct((B, MLA_HEADS, S, HEAD_SLOT), BF16),
                   jax.ShapeDtypeStruct((B, MLA_HEADS // 2, S, V7X_LANES), BF16)],
        compiler_params=_params("arbitrary"),
        name="qkv_proj",
    )(kvin, cq, gq, gkv, wq1, wq2, wkn, wkr, wv, qc, qs, kcs)


def _attn_body(*refs, n_sets):
    q_ref = refs[0]
    k_refs = refs[1:1 + 2 * n_sets:2]
    v_refs = refs[2:2 + 2 * n_sets:2]
    o_ref = refs[-1]
    tq = q_ref.shape[2]
    for hp in range(MLA_HEADS // 2):
        for r0 in range(0, tq, ATTN_Q_ROWS):
            rows = slice(r0, r0 + ATTN_Q_ROWS)
            outs = []
            for hh in range(2):
                h = 2 * hp + hh
                q = q_ref[0, h, rows, :]
                m = l = acc = None
                for k_ref, v_ref in zip(k_refs, v_refs):
                    for c0 in range(0, k_ref.shape[2], ATTN_KEY_CHUNK):
                        keys = slice(c0, c0 + ATTN_KEY_CHUNK)
                        s = lax.dot_general(q, k_ref[0, h, keys, :], (((1,), (1,)), ((), ())),
                                            preferred_element_type=F32)
                        m_c = jnp.max(s, axis=-1, keepdims=True)
                        m_new = m_c if m is None else jnp.maximum(m, m_c)
                        p = jnp.exp2(s - m_new)
                        l_c = functools.reduce(
                            jnp.add, [p[:, t:t + V7X_LANES] for t in range(0, ATTN_KEY_CHUNK, V7X_LANES)])
                        o_c = _dot(p.astype(BF16), v_ref[0, hp, keys, :])
                        if m is None:
                            l, acc = l_c, o_c
                        else:
                            alpha = jnp.exp2(m - m_new)
                            l, acc = alpha * l + l_c, alpha * acc + o_c
                        m = m_new
                outs.append(acc / jnp.sum(l, axis=-1, keepdims=True))
            lane = lax.broadcasted_iota(jnp.int32, outs[0].shape, 1)
            o_ref[0, rows, hp * V7X_LANES:(hp + 1) * V7X_LANES] = (
                jnp.where(lane < V_DIM, outs[0], outs[1]).astype(BF16))


def attention(q, kv_sets):
    B, H, S, _ = q.shape
    tq = TOKEN_TILE
    in_specs = [pl.BlockSpec((1, H, tq, HEAD_SLOT), lambda b, i: (b, 0, i, 0))]
    args = [q]
    for k, v in kv_sets:
        sk = k.shape[2]
        in_specs.append(pl.BlockSpec((1, H, sk, HEAD_SLOT), lambda b, i: (b, 0, 0, 0)))
        in_specs.append(pl.BlockSpec((1, H // 2, sk, V7X_LANES), lambda b, i: (b, 0, 0, 0)))
        args += [k, v]
    return pl.pallas_call(
        functools.partial(_attn_body, n_sets=len(kv_sets)),
        grid=(B, S // tq),
        in_specs=in_specs,
        out_specs=pl.BlockSpec((1, tq, H * V_DIM), lambda b, i: (b, i, 0)),
        out_shape=jax.ShapeDtypeStruct((B, S, H * V_DIM), BF16),
        compiler_params=_params("arbitrary", "arbitrary"),
        name="attention",
    )(*args)


def _fill_window(win_ref, x_ref, j, n_chunks, tc, halo):
    c = win_ref.shape[1]
    start = pl.multiple_of(j * tc, tc)
    win_ref[halo:halo + tc, :] = x_ref[0, pl.ds(start, tc), :]
    lo = pl.multiple_of(jnp.maximum(j * tc - halo, 0), halo)
    prev = x_ref[0, pl.ds(lo, halo), :]
    win_ref[0:halo, :] = jnp.where(j > 0, prev, jnp.zeros((halo, c), F32))
    hi = pl.multiple_of(jnp.minimum((j + 1) * tc, (n_chunks - 1) * tc + tc - halo), halo)
    nxt = x_ref[0, pl.ds(hi, halo), :]
    win_ref[halo + tc:2 * halo + tc, :] = jnp.where(j < n_chunks - 1, nxt, jnp.zeros((halo, c), F32))


def _conv_taps(win_ref, w_ref, b_ref, out_ref, tc, first_row, n_taps, row_block=64):
    c = win_ref.shape[1]
    for r0 in range(0, tc, row_block):
        for l0 in range(0, c, V7X_LANES):
            ls = slice(l0, l0 + V7X_LANES)
            acc = jnp.broadcast_to(b_ref[:, ls], (row_block, V7X_LANES))
            for k in range(n_taps):
                r = first_row + r0 + k
                acc = acc + win_ref[r:r + row_block, ls] * w_ref[k:k + 1, ls]
            out_ref[r0:r0 + row_block, ls] = acc


CONF_HALO = 16
CONF_ROW_BLOCK = 64


def _conf_body(u_ref, w_ref, b_ref, g_ref, beta_ref, o_ref, win_ref, sh_ref, acc_ref, *, n_chunks, tc):
    j = pl.program_id(1)
    _fill_window(win_ref, u_ref, j, n_chunks, tc, CONF_HALO)
    span = sh_ref.shape[1]
    for s in range(V7X_SUBLANES):
        sh_ref[s] = win_ref[s:s + span, :]
    first_row = CONF_HALO - (CONV_K - 1) // 2
    c = win_ref.shape[1]
    for r0 in range(0, tc, CONF_ROW_BLOCK):
        for l0 in range(0, c, V7X_LANES):
            ls = slice(l0, l0 + V7X_LANES)
            acc = jnp.broadcast_to(b_ref[:, ls], (CONF_ROW_BLOCK, V7X_LANES))
            for k in range(CONV_K):
                phase = (first_row + k) % V7X_SUBLANES
                base = r0 + (first_row + k) // V7X_SUBLANES * V7X_SUBLANES
                acc = acc + sh_ref[phase, base:base + CONF_ROW_BLOCK, ls] * w_ref[k:k + 1, ls]
            acc_ref[r0:r0 + CONF_ROW_BLOCK, ls] = acc
    y = _ln(acc_ref[...]) * g_ref[...] + beta_ref[...]
    o_ref[0] = (y * _sigmoid(y)).astype(BF16)


def conformer(u, w, b, g, beta):
    B, S, C = u.shape
    tc = TOKEN_TILE
    n_chunks = S // tc
    span = tc + 2 * CONF_HALO - V7X_SUBLANES
    return pl.pallas_call(
        functools.partial(_conf_body, n_chunks=n_chunks, tc=tc),
        grid=(B, n_chunks),
        in_specs=[pl.BlockSpec((1, S, C), lambda bb, j: (bb, 0, 0)),
                  _full_spec(w), _full_spec(b), _full_spec(g), _full_spec(beta)],
        out_specs=pl.BlockSpec((1, tc, C), lambda bb, j: (bb, j, 0)),
        out_shape=jax.ShapeDtypeStruct((B, S, C), BF16),
        scratch_shapes=[pltpu.VMEM((tc + 2 * CONF_HALO, C), F32), pltpu.VMEM((V7X_SUBLANES, span, C), F32),
                        pltpu.VMEM((tc, C), F32)],
        compiler_params=_params("arbitrary", "arbitrary"),
        name="conformer",
    )(u, w, b, g, beta)


def _fnet_body(c_ref, s_ref, a_ref, b_ref, o_ref, *, norm):
    y = _dot(c_ref[...], a_ref[0]) + _dot(s_ref[...], b_ref[0])
    o_ref[0] = (y * norm).astype(BF16)


def fnet_seq_dft(cmat, smat, fa, fb):
    B, S, W = fa.shape
    ts = min(S, 512)
    norm = 1.0 / math.sqrt(S * (FNET_WIDTH // FNET_GROUPS))
    return pl.pallas_call(
        functools.partial(_fnet_body, norm=norm),
        grid=(S // ts, B),
        in_specs=[pl.BlockSpec((ts, S), lambda i, b: (i, 0)), pl.BlockSpec((ts, S), lambda i, b: (i, 0)),
                  pl.BlockSpec((1, S, W), lambda i, b: (b, 0, 0)), pl.BlockSpec((1, S, W), lambda i, b: (b, 0, 0))],
        out_specs=pl.BlockSpec((1, ts, W), lambda i, b: (b, i, 0)),
        out_shape=jax.ShapeDtypeStruct((B, S, W), BF16),
        compiler_params=_params("arbitrary", "arbitrary"),
        name="fnet_seq_dft",
    )(cmat, smat, fa, fb)


LRU_HALO = 8
LRU_GROUP = V7X_SUBLANES
LRU_UNROLL = 8


def _lru_body(*refs, n_chunks, tc, reverse):
    if reverse:
        (x_ref, cw_ref, cb_ref, wg_ref, bg_ref, ca_ref, h0_ref, hf_ref, gb_ref,
         o_ref, s_ref, win_ref, xc_ref, a_ref, u_ref, carry_ref) = refs
    else:
        (x_ref, cw_ref, cb_ref, wg_ref, bg_ref, ca_ref, h0_ref,
         o_ref, s_ref, win_ref, xc_ref, a_ref, u_ref, carry_ref) = refs
    step = pl.program_id(1)
    j = (n_chunks - 1 - step) if reverse else step

    @pl.when(step == 0)
    def _():
        carry_ref[...] = jnp.broadcast_to(h0_ref[0], carry_ref.shape)

    _fill_window(win_ref, x_ref, j, n_chunks, tc, LRU_HALO)
    _conv_taps(win_ref, cw_ref, cb_ref, xc_ref, tc, LRU_HALO - LRU_CONV_K // 2, LRU_CONV_K)
    xc = xc_ref[...]
    gates = _dot(xc.astype(BF16), wg_ref[...]) + bg_ref[...]
    r = _sigmoid(gates[:, :LRU_WIDTH])
    i = _sigmoid(gates[:, LRU_WIDTH:])
    log_a = ca_ref[...] * r
    a = jnp.exp(log_a)
    a_ref[...] = a
    u_ref[...] = jnp.sqrt(-jnp.tanh(log_a) * (a * a + 1.0)) * (i * xc)

    n_groups = tc // LRU_GROUP
    row = lax.broadcasted_iota(jnp.int32, (LRU_GROUP, LRU_WIDTH), 0)

    def group(gi, carry):
        g = (n_groups - 1 - gi) if reverse else gi
        rows = pl.ds(pl.multiple_of(g * LRU_GROUP, LRU_GROUP), LRU_GROUP)
        a = a_ref[rows, :]
        u = u_ref[rows, :]
        for sh in (1, 2, 4):
            if reverse:
                a_sh = pltpu.roll(a, LRU_GROUP - sh, 0)
                u_sh = pltpu.roll(u, LRU_GROUP - sh, 0)
                valid = row < LRU_GROUP - sh
            else:
                a_sh = pltpu.roll(a, sh, 0)
                u_sh = pltpu.roll(u, sh, 0)
                valid = row >= sh
            a_sh = jnp.where(valid, a_sh, 1.0)
            u_sh = jnp.where(valid, u_sh, 0.0)
            u = a * u_sh + u
            a = a * a_sh
        h = u + a * carry
        u_ref[rows, :] = h
        last = h[0:1] if reverse else h[LRU_GROUP - 1:LRU_GROUP]
        return jnp.broadcast_to(last, h.shape)

    carry = lax.fori_loop(0, n_groups, group, carry_ref[...], unroll=LRU_UNROLL)
    carry_ref[...] = carry
    h = u_ref[...]
    if reverse:
        gb = gb_ref[0]
        gelu = 0.5 * gb * (1.0 + jnp.tanh(math.sqrt(2.0 / math.pi) * (gb + 0.044715 * (gb * gb * gb))))
        o_ref[0] = ((hf_ref[0] + h) * gelu).astype(BF16)
    else:
        o_ref[0] = h

    @pl.when(step == n_chunks - 1)
    def _():
        s_ref[0] = carry[0:1]


def lru_scan(xb, cw, cb, wg, bg, ca, h0, hf=None, gb=None):
    B, S, W = xb.shape
    reverse = hf is not None
    tc = TOKEN_TILE
    n_chunks = S // tc
    chunk = (lambda bb, s: (bb, n_chunks - 1 - s, 0)) if reverse else (lambda bb, s: (bb, s, 0))
    per_b = lambda bb, s: (bb, 0, 0)
    in_specs = [pl.BlockSpec((1, S, W), per_b), _full_spec(cw), _full_spec(cb), _full_spec(wg),
                _full_spec(bg), _full_spec(ca), pl.BlockSpec((1, 1, W), per_b)]
    args = [xb, cw, cb, wg, bg, ca, h0]
    if reverse:
        in_specs += [pl.BlockSpec((1, tc, W), chunk), pl.BlockSpec((1, tc, W), chunk)]
        args += [hf, gb]
    return pl.pallas_call(
        functools.partial(_lru_body, n_chunks=n_chunks, tc=tc, reverse=reverse),
        grid=(B, n_chunks),
        in_specs=in_specs,
        out_specs=[pl.BlockSpec((1, tc, W), chunk), pl.BlockSpec((1, 1, W), per_b)],
        out_shape=[jax.ShapeDtypeStruct((B, S, W), BF16 if reverse else F32),
                   jax.ShapeDtypeStruct((B, 1, W), F32)],
        scratch_shapes=[pltpu.VMEM((tc + 2 * LRU_HALO, W), F32), pltpu.VMEM((tc, W), F32),
                        pltpu.VMEM((tc, W), F32), pltpu.VMEM((tc, W), F32),
                        pltpu.VMEM((LRU_GROUP, W), F32)],
        compiler_params=_params("arbitrary", "arbitrary"),
        name="lru_bwd" if reverse else "lru_fwd",
    )(*args)


def _split_bf16(x):
    hi = x.astype(BF16)
    return hi, (x - hi.astype(F32)).astype(BF16)


def _merge_body(x_ref, sh1_ref, sc1_ref, g1_ref, sh2_ref, sc2_ref, attn_ref, conf_ref, fnet_ref, lru_ref,
                wgl_ref, wb_ref, wout_ref, lng_ref, lnb_ref, wrh_ref, wrl_ref,
                y_ref, h2_ref, aff_ref):
    x = x_ref[...]
    h = (_ln(x) * (1.0 + sc1_ref[0]) + sh1_ref[0]).astype(BF16)
    acc = None
    for k, br_ref in enumerate((attn_ref, conf_ref, fnet_ref, lru_ref)):
        gl = _dot(h, wgl_ref[:, k * D_MODEL:(k + 1) * D_MODEL])
        term = _sigmoid(gl) * _dot(br_ref[...], wb_ref[k])
        acc = term if acc is None else acc + term
    out = _dot(acc.astype(BF16), wout_ref[...])
    y = _ln(DEEPNORM_ALPHA * x + g1_ref[0] * out) * lng_ref[...] + lnb_ref[...]
    y_ref[...] = y
    h2 = _ln(y) * (1.0 + sc2_ref[0]) + sh2_ref[0]
    h2_ref[...] = h2.astype(BF16)
    hh, hl = _split_bf16(h2)
    logits = _dot(hh, wrh_ref[...]) + _dot(hh, wrl_ref[...]) + _dot(hl, wrh_ref[...])
    lane = lax.broadcasted_iota(jnp.int32, logits.shape, 1)
    valid = lane < N_EXPERTS
    logits = jnp.where(valid, logits, -jnp.inf)
    e = jnp.exp(logits - jnp.max(logits, axis=-1, keepdims=True))
    aff_ref[...] = e / jnp.sum(e, axis=-1, keepdims=True)


def merge(x, mods, branches, wgl, wb, wout, lng, lnb, wrh, wrl, rows_per_mod):
    M = x.shape[0]
    tm = min(MERGE_TILE, rows_per_mod)
    tiles_per_mod = rows_per_mod // tm
    row = lambda i: (i, 0)
    sh1, sc1, g1, sh2, sc2 = mods
    in_specs = ([pl.BlockSpec((tm, D_MODEL), row)] + [_mod_spec(m, tiles_per_mod) for m in mods]
                + [pl.BlockSpec((tm, BRANCH_WIDTH), row)] * N_BRANCH
                + [_full_spec(a) for a in (wgl, wb, wout, lng, lnb, wrh, wrl)])
    return pl.pallas_call(
        _merge_body,
        grid=(M // tm,),
        in_specs=in_specs,
        out_specs=[pl.BlockSpec((tm, D_MODEL), row), pl.BlockSpec((tm, D_MODEL), row),
                   pl.BlockSpec((tm, V7X_LANES), row)],
        out_shape=[jax.ShapeDtypeStruct((M, D_MODEL), F32), jax.ShapeDtypeStruct((M, D_MODEL), BF16),
                   jax.ShapeDtypeStruct((M, V7X_LANES), F32)],
        compiler_params=_params("arbitrary"),
        name="merge",
    )(x, sh1, sc1, g1, sh2, sc2, *branches, wgl, wb, wout, lng, lnb, wrh, wrl)


def _select_body(aff_ref, tri_ref, pos_ref, rng_ref, *, cap, n_tok, blk):
    bits = pltpu.bitcast(aff_ref[0], jnp.int32)

    def bisect(it, lo):
        cand = lo | jnp.left_shift(jnp.int32(1), 30 - it)
        cnt = jnp.sum(jnp.where(bits >= cand, 1.0, 0.0), axis=1, keepdims=True)
        return jnp.where(cnt >= cap, cand, lo)

    thr = lax.fori_loop(0, 31, bisect, jnp.zeros((N_EXPERTS, 1), jnp.int32))
    need = cap - jnp.sum(jnp.where(bits > thr, 1.0, 0.0), axis=1, keepdims=True)
    tri = tri_ref[...]
    ties_before = jnp.zeros((N_EXPERTS, 1), F32)
    sel_before = jnp.zeros((N_EXPERTS, 1), F32)
    lane = lax.broadcasted_iota(jnp.int32, (N_EXPERTS, V7X_LANES), 1)
    first_slot = jnp.zeros((N_EXPERTS, V7X_LANES), F32)
    for c0 in range(0, n_tok, V7X_LANES):
        sl = slice(c0, c0 + V7X_LANES)
        if c0 % blk == 0:
            first_slot = jnp.where(lane == c0 // blk, sel_before, first_slot)
        bits_c = bits[:, sl]
        gt_c = jnp.where(bits_c > thr, 1.0, 0.0)
        eq_c = jnp.where(bits_c == thr, 1.0, 0.0)
        tie_rank = _dot(eq_c.astype(BF16), tri) + ties_before
        ties_before = ties_before + jnp.sum(eq_c, axis=1, keepdims=True)
        sel_c = gt_c + eq_c * jnp.where(tie_rank < need, 1.0, 0.0)
        slot = _dot(sel_c.astype(BF16), tri) + sel_before
        sel_before = sel_before + jnp.sum(sel_c, axis=1, keepdims=True)
        pos_ref[0, :, sl] = jnp.where(sel_c > 0.0, slot, -1.0)

    rng_ref[0] = first_slot


def select_tokens(aff_t, tri, cap, blk):
    B, E, N = aff_t.shape
    return pl.pallas_call(
        functools.partial(_select_body, cap=cap, n_tok=N, blk=blk),
        grid=(B,),
        in_specs=[pl.BlockSpec((1, E, N), lambda b: (b, 0, 0)), _full_spec(tri)],
        out_specs=[pl.BlockSpec((1, E, N), lambda b: (b, 0, 0)),
                   pl.BlockSpec((1, E, V7X_LANES), lambda b: (b, 0, 0))],
        out_shape=[jax.ShapeDtypeStruct((B, E, N), F32), jax.ShapeDtypeStruct((B, E, V7X_LANES), F32)],
        compiler_params=_params("arbitrary"),
        name="select_tokens",
    )(aff_t, tri)


def _experts_body(first_ref, pos_ref, h_ref, wg_ref, wu_ref, wd_ref, y_ref, xe_ref, *, cap, blk, n_blk, win):
    e = pl.program_id(0)
    b = pl.program_id(1)
    base = (b * N_EXPERTS + e) * n_blk
    xe_ref[...] = jnp.zeros(xe_ref.shape, F32)
    for t in range(n_blk):
        first = first_ref[base + t]
        s0 = pl.multiple_of(lax.shift_left(lax.shift_right_logical(first, EXPERT_ALIGN_LOG2), EXPERT_ALIGN_LOG2),
                            1 << EXPERT_ALIGN_LOG2)
        slot = lax.broadcasted_iota(jnp.int32, (win, 1), 0) + s0
        pos = pos_ref[0, t].astype(jnp.int32)
        onehot = jnp.where(slot == pos, 1.0, 0.0).astype(BF16)
        xe_ref[pl.ds(s0, win), :] += _dot(onehot, h_ref[t * blk:(t + 1) * blk, :])
    xe = xe_ref[0:cap, :].astype(BF16)
    g = _dot(xe, wg_ref[0, 0])
    u = _dot(xe, wu_ref[0, 0])
    he = (g * _sigmoid(g) * u).astype(BF16)
    y_ref[0, 0] = _dot(he, wd_ref[0, 0]).astype(BF16)


def experts(first_slot, pos_blocks, h2, wg, wu, wd, layer, B, N, cap, blk):
    E = N_EXPERTS
    n_blk = N // blk
    win = min(blk, cap) + (1 << EXPERT_ALIGN_LOG2)
    grid_spec = pltpu.PrefetchScalarGridSpec(
        num_scalar_prefetch=1,
        grid=(E, B),
        in_specs=[pl.BlockSpec((1, n_blk, 1, blk), lambda e, b, r: (b * E + e, 0, 0, 0)),
                  pl.BlockSpec((N, D_MODEL), lambda e, b, r: (b, 0)),
                  pl.BlockSpec((1, 1, D_MODEL, EXPERT_FF), lambda e, b, r: (layer, e, 0, 0)),
                  pl.BlockSpec((1, 1, D_MODEL, EXPERT_FF), lambda e, b, r: (layer, e, 0, 0)),
                  pl.BlockSpec((1, 1, EXPERT_FF, D_MODEL), lambda e, b, r: (layer, e, 0, 0))],
        out_specs=pl.BlockSpec((1, 1, cap, D_MODEL), lambda e, b, r: (b, e, 0, 0)),
        scratch_shapes=[pltpu.VMEM((cap + win, D_MODEL), F32)])
    return pl.pallas_call(
        functools.partial(_experts_body, cap=cap, blk=blk, n_blk=n_blk, win=win),
        grid_spec=grid_spec,
        out_shape=jax.ShapeDtypeStruct((B, E, cap, D_MODEL), BF16),
        compiler_params=_params("arbitrary", "arbitrary"),
        name="experts",
    )(first_slot, pos_blocks, h2, wg, wu, wd)


def _combine_body(first_ref, x_ref, g2_ref, posc_ref, aff_ref, ye_ref, lng_ref, lnb_ref, o_ref,
                  *, cap, sub, win, tiles_per_batch, n_sub_batch):
    i = pl.program_id(0)
    b = i // tiles_per_batch
    sub_tiles = x_ref.shape[0] // sub
    first_sub = (i % tiles_per_batch) * sub_tiles
    lane_slot = lax.broadcasted_iota(jnp.int32, (1, win), 1)
    for st in range(sub_tiles):
        rows = slice(st * sub, (st + 1) * sub)
        posc = posc_ref[0, rows, :]
        aff = aff_ref[rows, :]
        acc = None
        for e in range(N_EXPERTS):
            first = first_ref[(b * N_EXPERTS + e) * n_sub_batch + first_sub + st]
            s0 = lax.shift_left(lax.shift_right_logical(first, EXPERT_ALIGN_LOG2), EXPERT_ALIGN_LOG2)
            s0 = pl.multiple_of(jnp.minimum(s0, cap - win), 1 << EXPERT_ALIGN_LOG2)
            pcol = posc[:, e:e + 1]
            onehot = jnp.where(pcol == (lane_slot + s0).astype(F32), 1.0, 0.0).astype(BF16)
            w = jnp.where(pcol >= 0.0, aff[:, e:e + 1], 0.0)
            term = w * _dot(onehot, ye_ref[0, e, pl.ds(s0, win), :])
            acc = term if acc is None else acc + term
        y = _ln(DEEPNORM_ALPHA * x_ref[rows, :] + g2_ref[0] * acc)
        o_ref[rows, :] = y * lng_ref[...] + lnb_ref[...]


def combine(first_slot, x, gate2, posc, aff, ye, lng, lnb, B, N, cap, sub):
    tm = min(N, 512)
    tpb = N // tm
    win = min(sub + (1 << EXPERT_ALIGN_LOG2), cap)
    row = lambda i, f: (i, 0)
    gate_spec = (pl.BlockSpec((1, 1, D_MODEL), lambda i, f: (0, 0, 0)) if gate2.shape[0] == 1
                 else pl.BlockSpec((1, 1, D_MODEL), lambda i, f: (i // tpb, 0, 0)))
    grid_spec = pltpu.PrefetchScalarGridSpec(
        num_scalar_prefetch=1,
        grid=(B * tpb,),
        in_specs=[pl.BlockSpec((tm, D_MODEL), row), gate_spec,
                  pl.BlockSpec((1, tm, V7X_LANES), lambda i, f: (i // tpb, i % tpb, 0)),
                  pl.BlockSpec((tm, V7X_LANES), row),
                  pl.BlockSpec((1, N_EXPERTS, cap, D_MODEL), lambda i, f: (i // tpb, 0, 0, 0)),
                  pl.BlockSpec(lng.shape, lambda i, f: (0, 0)), pl.BlockSpec(lnb.shape, lambda i, f: (0, 0))],
        out_specs=pl.BlockSpec((tm, D_MODEL), row))
    return pl.pallas_call(
        functools.partial(_combine_body, cap=cap, sub=sub, win=win, tiles_per_batch=tpb, n_sub_batch=N // sub),
        grid_spec=grid_spec,
        out_shape=jax.ShapeDtypeStruct((B * N, D_MODEL), F32),
        compiler_params=_params("arbitrary"),
        name="combine",
    )(first_slot, x, gate2, posc, aff, ye, lng, lnb)


def _rope_rotation():
    r = [[0.0] * QK_ROPE for _ in range(QK_ROPE)]
    half = QK_ROPE // 4
    for base in (0, QK_ROPE // 2):
        for j in range(half):
            r[base + j + half][base + j] = -1.0
            r[base + j][base + j + half] = 1.0
    return jnp.array(r, F32)


def _rope_tables(rows):
    row = jnp.repeat(jnp.arange(rows), GRID_W).astype(F32)
    col = jnp.tile(jnp.arange(GRID_W), rows).astype(F32)
    inv = ROPE_BASE ** (-jnp.arange(ROPE_AXIS_FREQ, dtype=F32) / ROPE_AXIS_FREQ)
    ang_r = row[:, None] * inv
    ang_c = col[:, None] * inv
    cos = jnp.concatenate([jnp.cos(ang_r)] * 2 + [jnp.cos(ang_c)] * 2, axis=-1)
    sin = jnp.concatenate([jnp.sin(ang_r)] * 2 + [jnp.sin(ang_c)] * 2, axis=-1)
    return cos, sin


def _position_tables(cos, sin):
    S = cos.shape[0]
    ones = jnp.ones((S, QK_NOPE), F32)
    pad = jnp.zeros((S, HEAD_SLOT - QK_NOPE - QK_ROPE), F32)
    qc = jnp.concatenate([ones, cos, pad], axis=-1)
    qs = jnp.concatenate([jnp.zeros((S, QK_NOPE), F32), sin, pad], axis=-1)
    kcs = jnp.concatenate([cos, sin, jnp.zeros((S, V7X_LANES - 2 * QK_ROPE), F32)], axis=-1)
    return qc, qs, kcs


def _dft_matrices(S):
    Q = 1 << (S.bit_length() // 2)
    P = S // Q
    j = jnp.arange(S, dtype=jnp.int32)[:, None]

    def table(step, n):
        ang = ((j * (step * jnp.arange(n, dtype=jnp.int32))[None, :]) % S).astype(F32) * (2.0 * math.pi / S)
        return jnp.cos(ang), jnp.sin(ang)

    c1, s1 = table(Q, P)
    c2, s2 = table(1, Q)
    cos = c1[:, :, None] * c2[:, None, :] - s1[:, :, None] * s2[:, None, :]
    sin = s1[:, :, None] * c2[:, None, :] + c1[:, :, None] * s2[:, None, :]
    return cos.reshape(S, S).astype(BF16), (-sin).reshape(S, S).astype(BF16)


def _channel_dft_weights():
    n = FNET_WIDTH // FNET_GROUPS
    k = jnp.arange(n, dtype=jnp.int32)
    ang = ((k[:, None] * k[None, :]) % n).astype(F32) * (2.0 * math.pi / n)
    eye = jnp.eye(FNET_GROUPS, dtype=F32)
    return jnp.concatenate([jnp.kron(eye, jnp.cos(ang)), jnp.kron(eye, jnp.sin(ang))], axis=1).astype(BF16)


def _block_diag_heads(w):
    H, d, _ = w.shape
    out = jnp.zeros((H, d, H, d), w.dtype)
    out = out.at[jnp.arange(H), :, jnp.arange(H), :].set(w)
    return out.reshape(H * d, H * d)


def _layer_weights(l, w_in, q_norm_g, w_uq, kv_norm_g, w_ukv, lru_wa, lru_ba, lru_wx, lru_bx, lru_lambda,
                   w_branch, w_out, w_router, rot):
    wi = w_in[l]
    o = 0
    ckv = wi[:, o:o + KV_LORA]
    o += KV_LORA
    kr = wi[:, o:o + QK_ROPE]
    o += QK_ROPE
    xb = wi[:, o:o + LRU_WIDTH]
    o += LRU_WIDTH
    cq = wi[:, o:o + Q_LORA]
    o += Q_LORA
    glu = wi[:, o:o + 2 * CONV_WIDTH]
    o += 2 * CONV_WIDTH
    f = wi[:, o:o + FNET_WIDTH]
    o += FNET_WIDTH
    gb = wi[:, o:o + LRU_WIDTH]
    o += LRU_WIDTH
    gl = wi[:, o:]
    kpad = jnp.zeros((D_MODEL, V7X_LANES - 2 * QK_ROPE), F32)
    w1 = jnp.concatenate([ckv, kr, kr @ rot, kpad, xb, cq, glu, f, gb], axis=1).astype(BF16)

    uq = w_uq[l]
    zq = jnp.zeros((Q_LORA, MLA_HEADS, HEAD_SLOT - QK_NOPE - QK_ROPE), F32)
    wq1 = jnp.concatenate([uq, zq], axis=-1).reshape(Q_LORA, MLA_HEADS * HEAD_SLOT).astype(BF16)
    uq_rot = jnp.einsum('rhd,de->rhe', uq[..., QK_NOPE:], rot)
    wq2 = jnp.concatenate([jnp.zeros((Q_LORA, MLA_HEADS, QK_NOPE), F32), uq_rot, zq], axis=-1)
    wq2 = wq2.reshape(Q_LORA, MLA_HEADS * HEAD_SLOT).astype(BF16)
    ukv = w_ukv[l]
    zk = jnp.zeros((KV_LORA, MLA_HEADS, HEAD_SLOT - QK_NOPE), F32)
    wkn = jnp.concatenate([ukv[..., :QK_NOPE], zk], axis=-1).reshape(KV_LORA, MLA_HEADS * HEAD_SLOT).astype(BF16)
    wv = ukv[..., QK_NOPE:].reshape(KV_LORA, MLA_HEADS * V_DIM).astype(BF16)
    eye = jnp.eye(QK_ROPE, dtype=F32)
    slot = jnp.concatenate([jnp.zeros((QK_ROPE, QK_NOPE), F32), eye,
                            jnp.zeros((QK_ROPE, HEAD_SLOT - QK_NOPE - QK_ROPE), F32)], axis=1)
    slot = jnp.tile(slot, (1, MLA_HEADS))
    wkr = jnp.concatenate([slot, slot, jnp.zeros((V7X_LANES - 2 * QK_ROPE, MLA_HEADS * HEAD_SLOT), F32)],
                          axis=0).astype(BF16)

    lru = []
    for d in range(2):
        wg = jnp.concatenate([_block_diag_heads(lru_wa[l, d]), _block_diag_heads(lru_wx[l, d])], axis=1).astype(BF16)
        bg = jnp.concatenate([lru_ba[l, d], lru_bx[l, d]])[None, :]
        ca = (-LRU_C * jax.nn.softplus(-lru_lambda[l, d]))[None, :]
        lru.append((wg, bg, ca))

    wr = jnp.pad(w_router[l], ((0, 0), (0, V7X_LANES - N_EXPERTS)))
    wrh, wrl = _split_bf16(wr)
    return dict(w1=w1, wq1=wq1, wq2=wq2, wkn=wkn, wkr=wkr, wv=wv, gq=q_norm_g[l][None, :],
                gkv=kv_norm_g[l][None, :], lru=lru, wgl=gl.astype(BF16), wb=w_branch[l].astype(BF16),
                wout=w_out[l].astype(BF16), wrh=wrh, wrl=wrl)


def _sequence_mixers(B, S, proj, lw, cv, lru_conv, tabs, dft, h0, need_branches=True):
    kvin, xb, cq, u, fa, fb, gb = proj
    q, k, v = qkv_proj(kvin, cq, lw['gq'], lw['gkv'], lw['wq1'], lw['wq2'], lw['wkn'], lw['wkr'], lw['wv'],
                       *tabs, B, S)
    conf = fnet = None
    if need_branches:
        conf = conformer(u.reshape(B, S, CONV_WIDTH), *cv)
        fnet = fnet_seq_dft(*dft, fa.reshape(B, S, FNET_WIDTH), fb.reshape(B, S, FNET_WIDTH))
    xb3 = xb.reshape(B, S, LRU_WIDTH)
    hf, s_fwd = lru_scan(xb3, *lru_conv, *lw['lru'][0], h0[0])
    hrec, s_bwd = lru_scan(xb3, *lru_conv, *lw['lru'][1], h0[1], hf=hf, gb=gb.reshape(B, S, LRU_WIDTH))
    return q, k, v, conf, fnet, hrec, (s_fwd, s_bwd)


def _moe(y, h2, aff, gate2, lw_e, ln2, tri, B, N):
    cap = EC_CAPACITY * N // N_EXPERTS
    aff_t = jnp.swapaxes(aff.reshape(B, N, V7X_LANES)[:, :, :N_EXPERTS], 1, 2)
    blk, sub = EXPERT_TOKEN_BLOCK, COMBINE_TOKEN_BLOCK
    pos, first_slot = select_tokens(aff_t, tri, cap, sub)
    first_slot = first_slot[:, :, :N // sub].astype(jnp.int32)
    first_blk = first_slot[:, :, ::blk // sub].reshape(-1)
    ye = experts(first_blk, pos.reshape(B * N_EXPERTS, N // blk, 1, blk), h2, *lw_e, B, N, cap, blk)
    posc = jnp.pad(jnp.swapaxes(pos, 1, 2), ((0, 0), (0, 0), (0, V7X_LANES - N_EXPERTS)), constant_values=-1.0)
    return combine(first_slot.reshape(-1), y, gate2, posc, aff, ye, *ln2, B, N, cap, sub)


def kernel(x, c, ctx, c_ctx, ada_w, ada_b, w_in, q_norm_g, w_uq, kv_norm_g, w_ukv, cv_w, cv_b, cv_ln_g, cv_ln_b, lru_conv_w, lru_conv_b, lru_wa, lru_ba, lru_wx, lru_bx, lru_lambda, w_branch, w_out, ln1_g, ln1_b, w_router, w_e_gate, w_e_up, w_e_down, ln2_g, ln2_b):
    B, S, _ = x.shape
    SC = ctx.shape[1]
    rot = _rope_rotation()
    cos, sin = _rope_tables(S // GRID_W)
    tabs_l = _position_tables(cos, sin)
    tabs_c = _position_tables(jnp.ones((SC, QK_ROPE), F32), jnp.zeros((SC, QK_ROPE), F32))
    dft_l = _dft_matrices(S)
    dft_c = _dft_matrices(SC)
    wcs = _channel_dft_weights()
    tri = jnp.triu(jnp.ones((V7X_LANES, V7X_LANES), F32), k=1).astype(BF16)
    zero_state = jnp.zeros((B, 1, LRU_WIDTH), F32)
    expert_w = (w_e_gate.astype(BF16), w_e_up.astype(BF16), w_e_down.astype(BF16))

    x_lat = x.reshape(B * S, D_MODEL)
    x_ctx = ctx.reshape(B * SC, D_MODEL)
    tpm_l = S
    tpm_c = SC
    for l in range(DEPTH):
        ctx_out = l < DEPTH - 1
        lw = _layer_weights(l, w_in, q_norm_g, w_uq, kv_norm_g, w_ukv, lru_wa, lru_ba, lru_wx, lru_bx,
                            lru_lambda, w_branch, w_out, w_router, rot)
        cv = (cv_w[l], cv_b[l][None, :], cv_ln_g[l][None, :], cv_ln_b[l][None, :])
        lru_conv = (lru_conv_w[l], lru_conv_b[l][None, :])
        ln1 = (ln1_g[l][None, :], ln1_b[l][None, :])
        ln2 = (ln2_g[l][None, :], ln2_b[l][None, :])
        lw_e = (*expert_w, l)

        mod = jax.nn.silu(c) @ ada_w[l] + ada_b[l]
        mods_l = [m[:, None, :] for m in jnp.split(mod, 6, axis=-1)]
        mod_c = jax.nn.silu(c_ctx) @ ada_w[l] + ada_b[l]
        mods_c = [m[None, None, :] for m in jnp.split(mod_c, 6)]

        proj_c = in_proj(x_ctx, mods_c[0], mods_c[1], lw['w1'], wcs, SC)
        q_c, k_c, v_c, conf_c, fnet_c, hrec_c, states = _sequence_mixers(
            B, SC, proj_c, lw, cv, lru_conv, tabs_c, dft_c, (zero_state, zero_state), need_branches=ctx_out)
        proj_l = in_proj(x_lat, mods_l[0], mods_l[1], lw['w1'], wcs, S)
        q_l, k_l, v_l, conf_l, fnet_l, hrec_l, _ = _sequence_mixers(
            B, S, proj_l, lw, cv, lru_conv, tabs_l, dft_l, states)

        attn_l = attention(q_l, [(k_l, v_l), (k_c, v_c)])
        merge_w = (lw['wgl'], lw['wb'], lw['wout'], *ln1, lw['wrh'], lw['wrl'])
        sel = lambda ms: (ms[0], ms[1], ms[2], ms[3], ms[4])
        y_l, h2_l, aff_l = merge(x_lat, sel(mods_l),
                                 [attn_l.reshape(B * S, -1), conf_l.reshape(B * S, -1),
                                  fnet_l.reshape(B * S, -1), hrec_l.reshape(B * S, -1)], *merge_w, tpm_l)
        x_lat = _moe(y_l, h2_l, aff_l, mods_l[5], lw_e, ln2, tri, B, S)
        if ctx_out:
            attn_c = attention(q_c, [(k_c, v_c)])
            y_c, h2_c, aff_c = merge(x_ctx, sel(mods_c),
                                     [attn_c.reshape(B * SC, -1), conf_c.reshape(B * SC, -1),
                                      fnet_c.reshape(B * SC, -1), hrec_c.reshape(B * SC, -1)], *merge_w, tpm_c)
            x_ctx = _moe(y_c, h2_c, aff_c, mods_c[5], lw_e, ln2, tri, B, SC)
    return x_lat.reshape(B, S, D_MODEL)
```

```python
import functools
import math

import jax
import jax.numpy as jnp
from jax import lax
from jax.experimental import pallas as pl
from jax.experimental.pallas import tpu as pltpu

D_MODEL = 1024
DEPTH = 4
GRID_W = 64
MLA_HEADS = 8
Q_LORA = 384
KV_LORA = 256
QK_NOPE = 64
QK_ROPE = 32
V_DIM = 64
ROPE_AXIS_FREQ = QK_ROPE // 4
ROPE_BASE = 10000.0
CONV_WIDTH = 512
CONV_K = 31
FNET_WIDTH = 512
FNET_GROUPS = 4
LRU_WIDTH = 512
LRU_HEADS = 8
LRU_CONV_K = 4
LRU_C = 8.0
N_BRANCH = 4
BRANCH_WIDTH = 512
N_EXPERTS = 16
EXPERT_FF = 1408
EC_CAPACITY = 2
DEEPNORM_ALPHA = (2 * DEPTH) ** 0.25
LN_EPS = 1e-6

V7X_LANES = 128
V7X_SUBLANES = 8
V7X_VMEM_LIMIT_BYTES = 56 * 1024 * 1024
HEAD_SLOT = V7X_LANES
TOKEN_TILE = 256
IN_PROJ_TILE = 512
MERGE_TILE = 256
SEQ_CHUNK = 512
ATTN_Q_ROWS = 128
ATTN_KEY_CHUNK = 256
EXPERT_TOKEN_BLOCK = 256
EXPERT_SMALL_FILL = 64
COMBINE_TOKEN_BLOCK = 128
EXPERT_ALIGN_LOG2 = 4
KVIN_WIDTH = KV_LORA + V7X_LANES
IN_PROJ_WIDTH = KVIN_WIDTH + LRU_WIDTH + Q_LORA + 2 * CONV_WIDTH + FNET_WIDTH + LRU_WIDTH

F32 = jnp.float32
BF16 = jnp.bfloat16


def _params(*sem):
    return pltpu.CompilerParams(dimension_semantics=sem, vmem_limit_bytes=V7X_VMEM_LIMIT_BYTES)


def _sigmoid(x):
    return 1.0 / (1.0 + jnp.exp(-x))


def _ln(x):
    mu = jnp.mean(x, axis=-1, keepdims=True)
    xc = x - mu
    var = jnp.mean(xc * xc, axis=-1, keepdims=True)
    return xc * lax.rsqrt(var + LN_EPS)


def _rms(x, g):
    return x * lax.rsqrt(jnp.mean(x * x, axis=-1, keepdims=True) + LN_EPS) * g


def _dot(a, b):
    return jnp.dot(a, b, preferred_element_type=F32)


def _mod_spec(mods, tiles_per_mod):
    if mods.shape[0] == 1:
        return pl.BlockSpec((1, 1, D_MODEL), lambda i: (0, 0, 0))
    return pl.BlockSpec((1, 1, D_MODEL), lambda i: (i // tiles_per_mod, 0, 0))


def _full_spec(arr):
    zeros = (0,) * arr.ndim
    return pl.BlockSpec(arr.shape, lambda *_: zeros)


def _in_proj_body(x_ref, sh_ref, sc_ref, w_ref, wcs_ref,
                  kvin_ref, xb_ref, cq_ref, u_ref, fa_ref, fb_ref, gb_ref):
    h = _ln(x_ref[...]) * (1.0 + sc_ref[0]) + sh_ref[0]
    p = _dot(h.astype(BF16), w_ref[...])
    o = 0
    kvin_ref[...] = p[:, o:o + KVIN_WIDTH]
    o += KVIN_WIDTH
    xb_ref[...] = p[:, o:o + LRU_WIDTH]
    o += LRU_WIDTH
    cq_ref[...] = p[:, o:o + Q_LORA]
    o += Q_LORA
    a = p[:, o:o + CONV_WIDTH]
    g = p[:, o + CONV_WIDTH:o + 2 * CONV_WIDTH]
    u_ref[...] = a * _sigmoid(g)
    o += 2 * CONV_WIDTH
    f = p[:, o:o + FNET_WIDTH].astype(BF16)
    o += FNET_WIDTH
    gb_ref[...] = p[:, o:o + LRU_WIDTH]
    ab = _dot(f, wcs_ref[...])
    fa_ref[...] = ab[:, :FNET_WIDTH].astype(BF16)
    fb_ref[...] = ab[:, FNET_WIDTH:].astype(BF16)


def in_proj(x, shift, scale, w1, wcs, rows_per_mod):
    M = x.shape[0]
    tm = min(IN_PROJ_TILE, rows_per_mod)
    tiles_per_mod = rows_per_mod // tm
    row = lambda i: (i, 0)
    widths = (KVIN_WIDTH, LRU_WIDTH, Q_LORA, CONV_WIDTH, FNET_WIDTH, FNET_WIDTH, LRU_WIDTH)
    dtypes = (F32, F32, F32, F32, BF16, BF16, F32)
    return pl.pallas_call(
        _in_proj_body,
        grid=(M // tm,),
        in_specs=[pl.BlockSpec((tm, D_MODEL), row), _mod_spec(shift, tiles_per_mod),
                  _mod_spec(scale, tiles_per_mod), _full_spec(w1), _full_spec(wcs)],
        out_specs=[pl.BlockSpec((tm, w), row) for w in widths],
        out_shape=[jax.ShapeDtypeStruct((M, w), d) for w, d in zip(widths, dtypes)],
        compiler_params=_params("arbitrary"),
        name="in_proj",
    )(x, shift, scale, w1, wcs)


def _qkv_body(kvin_ref, cq_ref, gq_ref, gkv_ref, wq1_ref, wq2_ref, wkn_ref, wkr_ref, wv_ref,
              qc_ref, qs_ref, kcs_ref, q_ref, k_ref, v_ref):
    nq = _rms(cq_ref[...], gq_ref[...]).astype(BF16)
    q1 = _dot(nq, wq1_ref[...])
    q2 = _dot(nq, wq2_ref[...])
    kvin = kvin_ref[...]
    n = _rms(kvin[:, :KV_LORA], gkv_ref[...]).astype(BF16)
    kr = (kvin[:, KV_LORA:] * kcs_ref[...]).astype(BF16)
    kk = _dot(n, wkn_ref[...]) + _dot(kr, wkr_ref[...])
    v = _dot(n, wv_ref[...])
    qc = qc_ref[...]
    qs = qs_ref[...]
    scale = (QK_NOPE + QK_ROPE) ** -0.5 * math.log2(math.e)
    for h in range(MLA_HEADS):
        sl = slice(h * HEAD_SLOT, (h + 1) * HEAD_SLOT)
        q_ref[0, h] = ((q1[:, sl] * qc + q2[:, sl] * qs) * scale).astype(BF16)
        k_ref[0, h] = kk[:, sl].astype(BF16)
    for p in range(MLA_HEADS // 2):
        v_ref[0, p] = v[:, p * V7X_LANES:(p + 1) * V7X_LANES].astype(BF16)


def qkv_proj(kvin, cq, gq, gkv, wq1, wq2, wkn, wkr, wv, qc, qs, kcs, B, S):
    tm = min(S, IN_PROJ_TILE)
    tps = S // tm
    row = lambda i: (i, 0)
    tab = lambda i: (i % tps, 0)
    seq = lambda i: (i // tps, 0, i % tps, 0)
    return pl.pallas_call(
        _qkv_body,
        grid=(B * tps,),
        in_specs=[pl.BlockSpec((tm, KVIN_WIDTH), row), pl.BlockSpec((tm, Q_LORA), row),
                  _full_spec(gq), _full_spec(gkv), _full_spec(wq1), _full_spec(wq2),
                  _full_spec(wkn), _full_spec(wkr), _full_spec(wv),
                  pl.BlockSpec((tm, HEAD_SLOT), tab), pl.BlockSpec((tm, HEAD_SLOT), tab),
                  pl.BlockSpec((tm, V7X_LANES), tab)],
        out_specs=[pl.BlockSpec((1, MLA_HEADS, tm, HEAD_SLOT), seq),
                   pl.BlockSpec((1, MLA_HEADS, tm, HEAD_SLOT), seq),
                   pl.BlockSpec((1, MLA_HEADS // 2, tm, V7X_LANES), seq)],
        out_shape=[jax.ShapeDtypeStruct((B, MLA_HEADS, S, HEAD_SLOT), BF16),
                   jax.ShapeDtypeStruct((B, MLA_HEADS, S, HEAD_SLOT), BF16),
                   jax.ShapeDtypeStruct((B, MLA_HEADS // 2, S, V7X_LANES), BF16)],
        compiler_params=_params("arbitrary"),
        name="qkv_proj",
    )(kvin, cq, gq, gkv, wq1, wq2, wkn, wkr, wv, qc, qs, kcs)


def _attn_body(*refs, n_sets):
    q_ref = refs[0]
    k_refs = refs[1:1 + 2 * n_sets:2]
    v_refs = refs[2:2 + 2 * n_sets:2]
    o_ref = refs[-1]
    tq = q_ref.shape[2]
    for hp in range(MLA_HEADS // 2):
        for r0 in range(0, tq, ATTN_Q_ROWS):
            rows = slice(r0, r0 + ATTN_Q_ROWS)
            outs = []
            for hh in range(2):
                h = 2 * hp + hh
                q = q_ref[0, h, rows, :]
                m = l = acc = None
                for k_ref, v_ref in zip(k_refs, v_refs):
                    for c0 in range(0, k_ref.shape[2], ATTN_KEY_CHUNK):
                        keys = slice(c0, c0 + ATTN_KEY_CHUNK)
                        s = lax.dot_general(q, k_ref[0, h, keys, :], (((1,), (1,)), ((), ())),
                                            preferred_element_type=F32)
                        m_c = jnp.max(s, axis=-1, keepdims=True)
                        m_new = m_c if m is None else jnp.maximum(m, m_c)
                        p = jnp.exp2(s - m_new)
                        l_c = functools.reduce(
                            jnp.add, [p[:, t:t + V7X_LANES] for t in range(0, ATTN_KEY_CHUNK, V7X_LANES)])
                        o_c = _dot(p.astype(BF16), v_ref[0, hp, keys, :])
                        if m is None:
                            l, acc = l_c, o_c
                        else:
                            alpha = jnp.exp2(m - m_new)
                            l, acc = alpha * l + l_c, alpha * acc + o_c
                        m = m_new
                outs.append(acc / jnp.sum(l, axis=-1, keepdims=True))
            lane = lax.broadcasted_iota(jnp.int32, outs[0].shape, 1)
            o_ref[0, rows, hp * V7X_LANES:(hp + 1) * V7X_LANES] = (
                jnp.where(lane < V_DIM, outs[0], outs[1]).astype(BF16))


def attention(q, kv_sets):
    B, H, S, _ = q.shape
    tq = TOKEN_TILE
    in_specs = [pl.BlockSpec((1, H, tq, HEAD_SLOT), lambda b, i: (b, 0, i, 0))]
    args = [q]
    for k, v in kv_sets:
        sk = k.shape[2]
        in_specs.append(pl.BlockSpec((1, H, sk, HEAD_SLOT), lambda b, i: (b, 0, 0, 0)))
        in_specs.append(pl.BlockSpec((1, H // 2, sk, V7X_LANES), lambda b, i: (b, 0, 0, 0)))
        args += [k, v]
    return pl.pallas_call(
        functools.partial(_attn_body, n_sets=len(kv_sets)),
        grid=(B, S // tq),
        in_specs=in_specs,
        out_specs=pl.BlockSpec((1, tq, H * V_DIM), lambda b, i: (b, i, 0)),
        out_shape=jax.ShapeDtypeStruct((B, S, H * V_DIM), BF16),
        compiler_params=_params("arbitrary", "arbitrary"),
        name="attention",
    )(*args)


def _fill_window(win_ref, x_ref, j, n_chunks, tc, halo):
    c = win_ref.shape[1]
    start = pl.multiple_of(j * tc, tc)
    win_ref[halo:halo + tc, :] = x_ref[0, pl.ds(start, tc), :]
    lo = pl.multiple_of(jnp.maximum(j * tc - halo, 0), halo)
    prev = x_ref[0, pl.ds(lo, halo), :]
    win_ref[0:halo, :] = jnp.where(j > 0, prev, jnp.zeros((halo, c), F32))
    hi = pl.multiple_of(jnp.minimum((j + 1) * tc, (n_chunks - 1) * tc + tc - halo), halo)
    nxt = x_ref[0, pl.ds(hi, halo), :]
    win_ref[halo + tc:2 * halo + tc, :] = jnp.where(j < n_chunks - 1, nxt, jnp.zeros((halo, c), F32))


def _conv_taps(win_ref, w_ref, b_ref, out_ref, tc, first_row, n_taps, row_block=64):
    c = win_ref.shape[1]
    for r0 in range(0, tc, row_block):
        for l0 in range(0, c, V7X_LANES):
            ls = slice(l0, l0 + V7X_LANES)
            acc = jnp.broadcast_to(b_ref[:, ls], (row_block, V7X_LANES))
            for k in range(n_taps):
                r = first_row + r0 + k
                acc = acc + win_ref[r:r + row_block, ls] * w_ref[k:k + 1, ls]
            out_ref[r0:r0 + row_block, ls] = acc


CONF_HALO = 16
CONF_ROW_BLOCK = 64


def _conf_body(u_ref, w_ref, b_ref, g_ref, beta_ref, o_ref, win_ref, sh_ref, acc_ref, *, n_chunks, tc):
    j = pl.program_id(1)
    _fill_window(win_ref, u_ref, j, n_chunks, tc, CONF_HALO)
    span = sh_ref.shape[1]
    for s in range(V7X_SUBLANES):
        sh_ref[s] = win_ref[s:s + span, :]
    first_row = CONF_HALO - (CONV_K - 1) // 2
    c = win_ref.shape[1]
    for r0 in range(0, tc, CONF_ROW_BLOCK):
        for l0 in range(0, c, V7X_LANES):
            ls = slice(l0, l0 + V7X_LANES)
            acc = jnp.broadcast_to(b_ref[:, ls], (CONF_ROW_BLOCK, V7X_LANES))
            for k in range(CONV_K):
                phase = (first_row + k) % V7X_SUBLANES
                base = r0 + (first_row + k) // V7X_SUBLANES * V7X_SUBLANES
                acc = acc + sh_ref[phase, base:base + CONF_ROW_BLOCK, ls] * w_ref[k:k + 1, ls]
            acc_ref[r0:r0 + CONF_ROW_BLOCK, ls] = acc
    y = _ln(acc_ref[...]) * g_ref[...] + beta_ref[...]
    o_ref[0] = (y * _sigmoid(y)).astype(BF16)


def conformer(u, w, b, g, beta):
    B, S, C = u.shape
    tc = min(S, SEQ_CHUNK)
    n_chunks = S // tc
    span = tc + 2 * CONF_HALO - V7X_SUBLANES
    return pl.pallas_call(
        functools.partial(_conf_body, n_chunks=n_chunks, tc=tc),
        grid=(B, n_chunks),
        in_specs=[pl.BlockSpec((1, S, C), lambda bb, j: (bb, 0, 0)),
                  _full_spec(w), _full_spec(b), _full_spec(g), _full_spec(beta)],
        out_specs=pl.BlockSpec((1, tc, C), lambda bb, j: (bb, j, 0)),
        out_shape=jax.ShapeDtypeStruct((B, S, C), BF16),
        scratch_shapes=[pltpu.VMEM((tc + 2 * CONF_HALO, C), F32), pltpu.VMEM((V7X_SUBLANES, span, C), F32),
                        pltpu.VMEM((tc, C), F32)],
        compiler_params=_params("arbitrary", "arbitrary"),
        name="conformer",
    )(u, w, b, g, beta)


def _fnet_body(c_ref, s_ref, a_ref, b_ref, o_ref, *, norm):
    y = _dot(c_ref[...], a_ref[0]) + _dot(s_ref[...], b_ref[0])
    o_ref[0] = (y * norm).astype(BF16)


def fnet_seq_dft(cmat, smat, fa, fb):
    B, S, W = fa.shape
    ts = min(S, 512)
    norm = 1.0 / math.sqrt(S * (FNET_WIDTH // FNET_GROUPS))
    return pl.pallas_call(
        functools.partial(_fnet_body, norm=norm),
        grid=(S // ts, B),
        in_specs=[pl.BlockSpec((ts, S), lambda i, b: (i, 0)), pl.BlockSpec((ts, S), lambda i, b: (i, 0)),
                  pl.BlockSpec((1, S, W), lambda i, b: (b, 0, 0)), pl.BlockSpec((1, S, W), lambda i, b: (b, 0, 0))],
        out_specs=pl.BlockSpec((1, ts, W), lambda i, b: (b, i, 0)),
        out_shape=jax.ShapeDtypeStruct((B, S, W), BF16),
        compiler_params=_params("arbitrary", "arbitrary"),
        name="fnet_seq_dft",
    )(cmat, smat, fa, fb)


LRU_HALO = 8
LRU_GROUP = V7X_SUBLANES
LRU_UNROLL = 8


def _lru_body(*refs, n_chunks, tc, reverse):
    if reverse:
        (x_ref, cw_ref, cb_ref, wg_ref, bg_ref, ca_ref, h0_ref, hf_ref, gb_ref,
         o_ref, s_ref, win_ref, xc_ref, a_ref, u_ref, carry_ref) = refs
    else:
        (x_ref, cw_ref, cb_ref, wg_ref, bg_ref, ca_ref, h0_ref,
         o_ref, s_ref, win_ref, xc_ref, a_ref, u_ref, carry_ref) = refs
    step = pl.program_id(1)
    j = (n_chunks - 1 - step) if reverse else step

    @pl.when(step == 0)
    def _():
        carry_ref[...] = jnp.broadcast_to(h0_ref[0], carry_ref.shape)

    _fill_window(win_ref, x_ref, j, n_chunks, tc, LRU_HALO)
    _conv_taps(win_ref, cw_ref, cb_ref, xc_ref, tc, LRU_HALO - LRU_CONV_K // 2, LRU_CONV_K)
    xc = xc_ref[...]
    gates = _dot(xc.astype(BF16), wg_ref[...]) + bg_ref[...]
    r = _sigmoid(gates[:, :LRU_WIDTH])
    i = _sigmoid(gates[:, LRU_WIDTH:])
    log_a = ca_ref[...] * r
    a = jnp.exp(log_a)
    a_ref[...] = a
    u_ref[...] = jnp.sqrt(-jnp.tanh(log_a) * (a * a + 1.0)) * (i * xc)

    n_groups = tc // LRU_GROUP
    row = lax.broadcasted_iota(jnp.int32, (LRU_GROUP, LRU_WIDTH), 0)

    def group(gi, carry):
        g = (n_groups - 1 - gi) if reverse else gi
        rows = pl.ds(pl.multiple_of(g * LRU_GROUP, LRU_GROUP), LRU_GROUP)
        a = a_ref[rows, :]
        u = u_ref[rows, :]
        for sh in (1, 2, 4):
            if reverse:
                a_sh = pltpu.roll(a, LRU_GROUP - sh, 0)
                u_sh = pltpu.roll(u, LRU_GROUP - sh, 0)
                valid = row < LRU_GROUP - sh
            else:
                a_sh = pltpu.roll(a, sh, 0)
                u_sh = pltpu.roll(u, sh, 0)
                valid = row >= sh
            a_sh = jnp.where(valid, a_sh, 1.0)
            u_sh = jnp.where(valid, u_sh, 0.0)
            u = a * u_sh + u
            a = a * a_sh
        h = u + a * carry
        u_ref[rows, :] = h
        last = h[0:1] if reverse else h[LRU_GROUP - 1:LRU_GROUP]
        return jnp.broadcast_to(last, h.shape)

    carry = lax.fori_loop(0, n_groups, group, carry_ref[...], unroll=LRU_UNROLL)
    carry_ref[...] = carry
    h = u_ref[...]
    if reverse:
        gb = gb_ref[0]
        gelu = 0.5 * gb * (1.0 + jnp.tanh(math.sqrt(2.0 / math.pi) * (gb + 0.044715 * (gb * gb * gb))))
        o_ref[0] = ((hf_ref[0] + h) * gelu).astype(BF16)
    else:
        o_ref[0] = h

    @pl.when(step == n_chunks - 1)
    def _():
        s_ref[0] = carry[0:1]


def lru_scan(xb, cw, cb, wg, bg, ca, h0, hf=None, gb=None):
    B, S, W = xb.shape
    reverse = hf is not None
    tc = min(S, SEQ_CHUNK)
    n_chunks = S // tc
    chunk = (lambda bb, s: (bb, n_chunks - 1 - s, 0)) if reverse else (lambda bb, s: (bb, s, 0))
    per_b = lambda bb, s: (bb, 0, 0)
    in_specs = [pl.BlockSpec((1, S, W), per_b), _full_spec(cw), _full_spec(cb), _full_spec(wg),
                _full_spec(bg), _full_spec(ca), pl.BlockSpec((1, 1, W), per_b)]
    args = [xb, cw, cb, wg, bg, ca, h0]
    if reverse:
        in_specs += [pl.BlockSpec((1, tc, W), chunk), pl.BlockSpec((1, tc, W), chunk)]
        args += [hf, gb]
    return pl.pallas_call(
        functools.partial(_lru_body, n_chunks=n_chunks, tc=tc, reverse=reverse),
        grid=(B, n_chunks),
        in_specs=in_specs,
        out_specs=[pl.BlockSpec((1, tc, W), chunk), pl.BlockSpec((1, 1, W), per_b)],
        out_shape=[jax.ShapeDtypeStruct((B, S, W), BF16 if reverse else F32),
                   jax.ShapeDtypeStruct((B, 1, W), F32)],
        scratch_shapes=[pltpu.VMEM((tc + 2 * LRU_HALO, W), F32), pltpu.VMEM((tc, W), F32),
                        pltpu.VMEM((tc, W), F32), pltpu.VMEM((tc, W), F32),
                        pltpu.VMEM((LRU_GROUP, W), F32)],
        compiler_params=_params("arbitrary", "arbitrary"),
        name="lru_bwd" if reverse else "lru_fwd",
    )(*args)


def _split_bf16(x):
    hi = x.astype(BF16)
    return hi, (x - hi.astype(F32)).astype(BF16)


def _merge_body(x_ref, sh1_ref, sc1_ref, g1_ref, sh2_ref, sc2_ref, attn_ref, conf_ref, fnet_ref, lru_ref,
                wgl_ref, wb_ref, wout_ref, lng_ref, lnb_ref, wrh_ref, wrl_ref,
                y_ref, h2_ref, aff_ref):
    x = x_ref[...]
    h = (_ln(x) * (1.0 + sc1_ref[0]) + sh1_ref[0]).astype(BF16)
    acc = None
    for k, br_ref in enumerate((attn_ref, conf_ref, fnet_ref, lru_ref)):
        gl = _dot(h, wgl_ref[:, k * D_MODEL:(k + 1) * D_MODEL])
        term = _sigmoid(gl) * _dot(br_ref[...], wb_ref[k])
        acc = term if acc is None else acc + term
    out = _dot(acc.astype(BF16), wout_ref[...])
    y = _ln(DEEPNORM_ALPHA * x + g1_ref[0] * out) * lng_ref[...] + lnb_ref[...]
    y_ref[...] = y
    h2 = _ln(y) * (1.0 + sc2_ref[0]) + sh2_ref[0]
    h2_ref[...] = h2.astype(BF16)
    hh, hl = _split_bf16(h2)
    logits = _dot(hh, wrh_ref[...]) + _dot(hh, wrl_ref[...]) + _dot(hl, wrh_ref[...])
    lane = lax.broadcasted_iota(jnp.int32, logits.shape, 1)
    valid = lane < N_EXPERTS
    logits = jnp.where(valid, logits, -jnp.inf)
    e = jnp.exp(logits - jnp.max(logits, axis=-1, keepdims=True))
    aff_ref[...] = e / jnp.sum(e, axis=-1, keepdims=True)


def merge(x, mods, branches, wgl, wb, wout, lng, lnb, wrh, wrl, rows_per_mod):
    M = x.shape[0]
    tm = min(MERGE_TILE, rows_per_mod)
    tiles_per_mod = rows_per_mod // tm
    row = lambda i: (i, 0)
    sh1, sc1, g1, sh2, sc2 = mods
    in_specs = ([pl.BlockSpec((tm, D_MODEL), row)] + [_mod_spec(m, tiles_per_mod) for m in mods]
                + [pl.BlockSpec((tm, BRANCH_WIDTH), row)] * N_BRANCH
                + [_full_spec(a) for a in (wgl, wb, wout, lng, lnb, wrh, wrl)])
    return pl.pallas_call(
        _merge_body,
        grid=(M // tm,),
        in_specs=in_specs,
        out_specs=[pl.BlockSpec((tm, D_MODEL), row), pl.BlockSpec((tm, D_MODEL), row),
                   pl.BlockSpec((tm, V7X_LANES), row)],
        out_shape=[jax.ShapeDtypeStruct((M, D_MODEL), F32), jax.ShapeDtypeStruct((M, D_MODEL), BF16),
                   jax.ShapeDtypeStruct((M, V7X_LANES), F32)],
        compiler_params=_params("arbitrary"),
        name="merge",
    )(x, sh1, sc1, g1, sh2, sc2, *branches, wgl, wb, wout, lng, lnb, wrh, wrl)


def _select_body(aff_ref, tri_ref, pos_ref, rng_ref, *, cap, n_tok, blk):
    bits = pltpu.bitcast(aff_ref[0], jnp.int32)

    def bisect(it, lo):
        cand = lo | jnp.left_shift(jnp.int32(1), 30 - it)
        cnt = jnp.sum(jnp.where(bits >= cand, 1.0, 0.0), axis=1, keepdims=True)
        return jnp.where(cnt >= cap, cand, lo)

    thr = lax.fori_loop(0, 31, bisect, jnp.zeros((N_EXPERTS, 1), jnp.int32))
    need = cap - jnp.sum(jnp.where(bits > thr, 1.0, 0.0), axis=1, keepdims=True)
    tri = tri_ref[...]
    ties_before = jnp.zeros((N_EXPERTS, 1), F32)
    sel_before = jnp.zeros((N_EXPERTS, 1), F32)
    lane = lax.broadcasted_iota(jnp.int32, (N_EXPERTS, V7X_LANES), 1)
    first_slot = jnp.zeros((N_EXPERTS, V7X_LANES), F32)
    for c0 in range(0, n_tok, V7X_LANES):
        sl = slice(c0, c0 + V7X_LANES)
        if c0 % blk == 0:
            first_slot = jnp.where(lane == c0 // blk, sel_before, first_slot)
        bits_c = bits[:, sl]
        gt_c = jnp.where(bits_c > thr, 1.0, 0.0)
        eq_c = jnp.where(bits_c == thr, 1.0, 0.0)
        tie_rank = _dot(eq_c.astype(BF16), tri) + ties_before
        ties_before = ties_before + jnp.sum(eq_c, axis=1, keepdims=True)
        sel_c = gt_c + eq_c * jnp.where(tie_rank < need, 1.0, 0.0)
        slot = _dot(sel_c.astype(BF16), tri) + sel_before
        sel_before = sel_before + jnp.sum(sel_c, axis=1, keepdims=True)
        pos_ref[0, :, sl] = jnp.where(sel_c > 0.0, slot, -1.0)

    rng_ref[0] = first_slot


def select_tokens(aff_t, tri, cap, blk):
    B, E, N = aff_t.shape
    return pl.pallas_call(
        functools.partial(_select_body, cap=cap, n_tok=N, blk=blk),
        grid=(B,),
        in_specs=[pl.BlockSpec((1, E, N), lambda b: (b, 0, 0)), _full_spec(tri)],
        out_specs=[pl.BlockSpec((1, E, N), lambda b: (b, 0, 0)),
                   pl.BlockSpec((1, E, V7X_LANES), lambda b: (b, 0, 0))],
        out_shape=[jax.ShapeDtypeStruct((B, E, N), F32), jax.ShapeDtypeStruct((B, E, V7X_LANES), F32)],
        compiler_params=_params("arbitrary"),
        name="select_tokens",
    )(aff_t, tri)


def _experts_body(first_ref, pos_ref, h_ref, wg_ref, wu_ref, wd_ref, y_ref, xe_ref,
                  *, cap, blk, n_blk, win, win_small):
    e = pl.program_id(0)
    b = pl.program_id(1)
    base = (b * N_EXPERTS + e) * n_blk
    xe_ref[...] = jnp.zeros(xe_ref.shape, F32)
    firsts = [first_ref[base + t] for t in range(n_blk)]
    fills = [nxt - cur for cur, nxt in zip(firsts, firsts[1:] + [cap])]

    def gather(rows):
        for t in range(n_blk):
            s0 = lax.shift_left(lax.shift_right_logical(firsts[t], EXPERT_ALIGN_LOG2), EXPERT_ALIGN_LOG2)
            s0 = pl.multiple_of(s0, 1 << EXPERT_ALIGN_LOG2)
            slot = lax.broadcasted_iota(jnp.int32, (rows, 1), 0) + s0
            pos = pos_ref[0, t].astype(jnp.int32)
            onehot = jnp.where(slot == pos, 1.0, 0.0).astype(BF16)
            xe_ref[pl.ds(s0, rows), :] += _dot(onehot, h_ref[t * blk:(t + 1) * blk, :])

    if win_small < win:
        narrow = functools.reduce(jnp.maximum, fills) <= EXPERT_SMALL_FILL
        pl.when(narrow)(lambda: gather(win_small))
        pl.when(jnp.logical_not(narrow))(lambda: gather(win))
    else:
        gather(win)
    xe = xe_ref[0:cap, :].astype(BF16)
    g = _dot(xe, wg_ref[0, 0])
    u = _dot(xe, wu_ref[0, 0])
    he = (g * _sigmoid(g) * u).astype(BF16)
    y_ref[0, 0] = _dot(he, wd_ref[0, 0]).astype(BF16)


def experts(first_slot, pos_blocks, h2, wg, wu, wd, layer, B, N, cap, blk):
    E = N_EXPERTS
    n_blk = N // blk
    win = min(blk, cap) + (1 << EXPERT_ALIGN_LOG2)
    win_small = EXPERT_SMALL_FILL + (1 << EXPERT_ALIGN_LOG2)
    grid_spec = pltpu.PrefetchScalarGridSpec(
        num_scalar_prefetch=1,
        grid=(E, B),
        in_specs=[pl.BlockSpec((1, n_blk, 1, blk), lambda e, b, r: (b * E + e, 0, 0, 0)),
                  pl.BlockSpec((N, D_MODEL), lambda e, b, r: (b, 0)),
                  pl.BlockSpec((1, 1, D_MODEL, EXPERT_FF), lambda e, b, r: (layer, e, 0, 0)),
                  pl.BlockSpec((1, 1, D_MODEL, EXPERT_FF), lambda e, b, r: (layer, e, 0, 0)),
                  pl.BlockSpec((1, 1, EXPERT_FF, D_MODEL), lambda e, b, r: (layer, e, 0, 0))],
        out_specs=pl.BlockSpec((1, 1, cap, D_MODEL), lambda e, b, r: (b, e, 0, 0)),
        scratch_shapes=[pltpu.VMEM((cap + win, D_MODEL), F32)])
    return pl.pallas_call(
        functools.partial(_experts_body, cap=cap, blk=blk, n_blk=n_blk, win=win, win_small=win_small),
        grid_spec=grid_spec,
        out_shape=jax.ShapeDtypeStruct((B, E, cap, D_MODEL), BF16),
        compiler_params=_params("arbitrary", "arbitrary"),
        name="experts",
    )(first_slot, pos_blocks, h2, wg, wu, wd)


def _combine_body(first_ref, x_ref, g2_ref, posc_ref, aff_ref, ye_ref, lng_ref, lnb_ref, o_ref,
                  *, cap, sub, win, tiles_per_batch, n_sub_batch):
    i = pl.program_id(0)
    b = i // tiles_per_batch
    sub_tiles = x_ref.shape[0] // sub
    first_sub = (i % tiles_per_batch) * sub_tiles
    lane_slot = lax.broadcasted_iota(jnp.int32, (1, win), 1)
    for st in range(sub_tiles):
        rows = slice(st * sub, (st + 1) * sub)
        posc = posc_ref[0, rows, :]
        aff = aff_ref[rows, :]
        acc = None
        for e in range(N_EXPERTS):
            first = first_ref[(b * N_EXPERTS + e) * n_sub_batch + first_sub + st]
            s0 = lax.shift_left(lax.shift_right_logical(first, EXPERT_ALIGN_LOG2), EXPERT_ALIGN_LOG2)
            s0 = pl.multiple_of(jnp.minimum(s0, cap - win), 1 << EXPERT_ALIGN_LOG2)
            pcol = posc[:, e:e + 1]
            onehot = jnp.where(pcol == (lane_slot + s0).astype(F32), 1.0, 0.0).astype(BF16)
            w = jnp.where(pcol >= 0.0, aff[:, e:e + 1], 0.0)
            term = w * _dot(onehot, ye_ref[0, e, pl.ds(s0, win), :])
            acc = term if acc is None else acc + term
        y = _ln(DEEPNORM_ALPHA * x_ref[rows, :] + g2_ref[0] * acc)
        o_ref[rows, :] = y * lng_ref[...] + lnb_ref[...]


def combine(first_slot, x, gate2, posc, aff, ye, lng, lnb, B, N, cap, sub):
    tm = min(N, 512)
    tpb = N // tm
    win = min(sub + (1 << EXPERT_ALIGN_LOG2), cap)
    row = lambda i, f: (i, 0)
    gate_spec = (pl.BlockSpec((1, 1, D_MODEL), lambda i, f: (0, 0, 0)) if gate2.shape[0] == 1
                 else pl.BlockSpec((1, 1, D_MODEL), lambda i, f: (i // tpb, 0, 0)))
    grid_spec = pltpu.PrefetchScalarGridSpec(
        num_scalar_prefetch=1,
        grid=(B * tpb,),
        in_specs=[pl.BlockSpec((tm, D_MODEL), row), gate_spec,
                  pl.BlockSpec((1, tm, V7X_LANES), lambda i, f: (i // tpb, i % tpb, 0)),
                  pl.BlockSpec((tm, V7X_LANES), row),
                  pl.BlockSpec((1, N_EXPERTS, cap, D_MODEL), lambda i, f: (i // tpb, 0, 0, 0)),
                  pl.BlockSpec(lng.shape, lambda i, f: (0, 0)), pl.BlockSpec(lnb.shape, lambda i, f: (0, 0))],
        out_specs=pl.BlockSpec((tm, D_MODEL), row))
    return pl.pallas_call(
        functools.partial(_combine_body, cap=cap, sub=sub, win=win, tiles_per_batch=tpb, n_sub_batch=N // sub),
        grid_spec=grid_spec,
        out_shape=jax.ShapeDtypeStruct((B * N, D_MODEL), F32),
        compiler_params=_params("arbitrary"),
        name="combine",
    )(first_slot, x, gate2, posc, aff, ye, lng, lnb)


def _rope_rotation():
    r = [[0.0] * QK_ROPE for _ in range(QK_ROPE)]
    half = QK_ROPE // 4
    for base in (0, QK_ROPE // 2):
        for j in range(half):
            r[base + j + half][base + j] = -1.0
            r[base + j][base + j + half] = 1.0
    return jnp.array(r, F32)


def _rope_tables(rows):
    row = jnp.repeat(jnp.arange(rows), GRID_W).astype(F32)
    col = jnp.tile(jnp.arange(GRID_W), rows).astype(F32)
    inv = ROPE_BASE ** (-jnp.arange(ROPE_AXIS_FREQ, dtype=F32) / ROPE_AXIS_FREQ)
    ang_r = row[:, None] * inv
    ang_c = col[:, None] * inv
    cos = jnp.concatenate([jnp.cos(ang_r)] * 2 + [jnp.cos(ang_c)] * 2, axis=-1)
    sin = jnp.concatenate([jnp.sin(ang_r)] * 2 + [jnp.sin(ang_c)] * 2, axis=-1)
    return cos, sin


def _position_tables(cos, sin):
    S = cos.shape[0]
    ones = jnp.ones((S, QK_NOPE), F32)
    pad = jnp.zeros((S, HEAD_SLOT - QK_NOPE - QK_ROPE), F32)
    qc = jnp.concatenate([ones, cos, pad], axis=-1)
    qs = jnp.concatenate([jnp.zeros((S, QK_NOPE), F32), sin, pad], axis=-1)
    kcs = jnp.concatenate([cos, sin, jnp.zeros((S, V7X_LANES - 2 * QK_ROPE), F32)], axis=-1)
    return qc, qs, kcs


def _dft_matrices(S):
    Q = 1 << (S.bit_length() // 2)
    P = S // Q
    j = jnp.arange(S, dtype=jnp.int32)[:, None]

    def table(step, n):
        ang = ((j * (step * jnp.arange(n, dtype=jnp.int32))[None, :]) % S).astype(F32) * (2.0 * math.pi / S)
        return jnp.cos(ang), jnp.sin(ang)

    c1, s1 = table(Q, P)
    c2, s2 = table(1, Q)
    cos = c1[:, :, None] * c2[:, None, :] - s1[:, :, None] * s2[:, None, :]
    sin = s1[:, :, None] * c2[:, None, :] + c1[:, :, None] * s2[:, None, :]
    return cos.reshape(S, S).astype(BF16), (-sin).reshape(S, S).astype(BF16)


def _channel_dft_weights():
    n = FNET_WIDTH // FNET_GROUPS
    k = jnp.arange(n, dtype=jnp.int32)
    ang = ((k[:, None] * k[None, :]) % n).astype(F32) * (2.0 * math.pi / n)
    eye = jnp.eye(FNET_GROUPS, dtype=F32)
    return jnp.concatenate([jnp.kron(eye, jnp.cos(ang)), jnp.kron(eye, jnp.sin(ang))], axis=1).astype(BF16)


def _block_diag_heads(w):
    H, d, _ = w.shape
    out = jnp.zeros((H, d, H, d), w.dtype)
    out = out.at[jnp.arange(H), :, jnp.arange(H), :].set(w)
    return out.reshape(H * d, H * d)


def _layer_weights(l, w_in, q_norm_g, w_uq, kv_norm_g, w_ukv, lru_wa, lru_ba, lru_wx, lru_bx, lru_lambda,
                   w_branch, w_out, w_router, rot):
    wi = w_in[l]
    o = 0
    ckv = wi[:, o:o + KV_LORA]
    o += KV_LORA
    kr = wi[:, o:o + QK_ROPE]
    o += QK_ROPE
    xb = wi[:, o:o + LRU_WIDTH]
    o += LRU_WIDTH
    cq = wi[:, o:o + Q_LORA]
    o += Q_LORA
    glu = wi[:, o:o + 2 * CONV_WIDTH]
    o += 2 * CONV_WIDTH
    f = wi[:, o:o + FNET_WIDTH]
    o += FNET_WIDTH
    gb = wi[:, o:o + LRU_WIDTH]
    o += LRU_WIDTH
    gl = wi[:, o:]
    kpad = jnp.zeros((D_MODEL, V7X_LANES - 2 * QK_ROPE), F32)
    w1 = jnp.concatenate([ckv, kr, kr @ rot, kpad, xb, cq, glu, f, gb], axis=1).astype(BF16)

    uq = w_uq[l]
    zq = jnp.zeros((Q_LORA, MLA_HEADS, HEAD_SLOT - QK_NOPE - QK_ROPE), F32)
    wq1 = jnp.concatenate([uq, zq], axis=-1).reshape(Q_LORA, MLA_HEADS * HEAD_SLOT).astype(BF16)
    uq_rot = jnp.einsum('rhd,de->rhe', uq[..., QK_NOPE:], rot)
    wq2 = jnp.concatenate([jnp.zeros((Q_LORA, MLA_HEADS, QK_NOPE), F32), uq_rot, zq], axis=-1)
    wq2 = wq2.reshape(Q_LORA, MLA_HEADS * HEAD_SLOT).astype(BF16)
    ukv = w_ukv[l]
    zk = jnp.zeros((KV_LORA, MLA_HEADS, HEAD_SLOT - QK_NOPE), F32)
    wkn = jnp.concatenate([ukv[..., :QK_NOPE], zk], axis=-1).reshape(KV_LORA, MLA_HEADS * HEAD_SLOT).astype(BF16)
    wv = ukv[..., QK_NOPE:].reshape(KV_LORA, MLA_HEADS * V_DIM).astype(BF16)
    eye = jnp.eye(QK_ROPE, dtype=F32)
    slot = jnp.concatenate([jnp.zeros((QK_ROPE, QK_NOPE), F32), eye,
                            jnp.zeros((QK_ROPE, HEAD_SLOT - QK_NOPE - QK_ROPE), F32)], axis=1)
    slot = jnp.tile(slot, (1, MLA_HEADS))
    wkr = jnp.concatenate([slot, slot, jnp.zeros((V7X_LANES - 2 * QK_ROPE, MLA_HEADS * HEAD_SLOT), F32)],
                          axis=0).astype(BF16)

    lru = []
    for d in range(2):
        wg = jnp.concatenate([_block_diag_heads(lru_wa[l, d]), _block_diag_heads(lru_wx[l, d])], axis=1).astype(BF16)
        bg = jnp.concatenate([lru_ba[l, d], lru_bx[l, d]])[None, :]
        ca = (-LRU_C * jax.nn.softplus(-lru_lambda[l, d]))[None, :]
        lru.append((wg, bg, ca))

    wr = jnp.pad(w_router[l], ((0, 0), (0, V7X_LANES - N_EXPERTS)))
    wrh, wrl = _split_bf16(wr)
    return dict(w1=w1, wq1=wq1, wq2=wq2, wkn=wkn, wkr=wkr, wv=wv, gq=q_norm_g[l][None, :],
                gkv=kv_norm_g[l][None, :], lru=lru, wgl=gl.astype(BF16), wb=w_branch[l].astype(BF16),
                wout=w_out[l].astype(BF16), wrh=wrh, wrl=wrl)


def _sequence_mixers(B, S, proj, lw, cv, lru_conv, tabs, dft, h0, need_branches=True):
    kvin, xb, cq, u, fa, fb, gb = proj
    q, k, v = qkv_proj(kvin, cq, lw['gq'], lw['gkv'], lw['wq1'], lw['wq2'], lw['wkn'], lw['wkr'], lw['wv'],
                       *tabs, B, S)
    conf = fnet = None
    if need_branches:
        conf = conformer(u.reshape(B, S, CONV_WIDTH), *cv)
        fnet = fnet_seq_dft(*dft, fa.reshape(B, S, FNET_WIDTH), fb.reshape(B, S, FNET_WIDTH))
    xb3 = xb.reshape(B, S, LRU_WIDTH)
    hf, s_fwd = lru_scan(xb3, *lru_conv, *lw['lru'][0], h0[0])
    hrec, s_bwd = lru_scan(xb3, *lru_conv, *lw['lru'][1], h0[1], hf=hf, gb=gb.reshape(B, S, LRU_WIDTH))
    return q, k, v, conf, fnet, hrec, (s_fwd, s_bwd)


def _moe(y, h2, aff, gate2, lw_e, ln2, tri, B, N):
    cap = EC_CAPACITY * N // N_EXPERTS
    aff_t = jnp.swapaxes(aff.reshape(B, N, V7X_LANES)[:, :, :N_EXPERTS], 1, 2)
    blk, sub = EXPERT_TOKEN_BLOCK, COMBINE_TOKEN_BLOCK
    pos, first_slot = select_tokens(aff_t, tri, cap, sub)
    first_slot = first_slot[:, :, :N // sub].astype(jnp.int32)
    first_blk = first_slot[:, :, ::blk // sub].reshape(-1)
    ye = experts(first_blk, pos.reshape(B * N_EXPERTS, N // blk, 1, blk), h2, *lw_e, B, N, cap, blk)
    posc = jnp.pad(jnp.swapaxes(pos, 1, 2), ((0, 0), (0, 0), (0, V7X_LANES - N_EXPERTS)), constant_values=-1.0)
    return combine(first_slot.reshape(-1), y, gate2, posc, aff, ye, *ln2, B, N, cap, sub)


def kernel(x, c, ctx, c_ctx, ada_w, ada_b, w_in, q_norm_g, w_uq, kv_norm_g, w_ukv, cv_w, cv_b, cv_ln_g, cv_ln_b, lru_conv_w, lru_conv_b, lru_wa, lru_ba, lru_wx, lru_bx, lru_lambda, w_branch, w_out, ln1_g, ln1_b, w_router, w_e_gate, w_e_up, w_e_down, ln2_g, ln2_b):
    B, S, _ = x.shape
    SC = ctx.shape[1]
    rot = _rope_rotation()
    cos, sin = _rope_tables(S // GRID_W)
    tabs_l = _position_tables(cos, sin)
    tabs_c = _position_tables(jnp.ones((SC, QK_ROPE), F32), jnp.zeros((SC, QK_ROPE), F32))
    dft_l = _dft_matrices(S)
    dft_c = _dft_matrices(SC)
    wcs = _channel_dft_weights()
    tri = jnp.triu(jnp.ones((V7X_LANES, V7X_LANES), F32), k=1).astype(BF16)
    zero_state = jnp.zeros((B, 1, LRU_WIDTH), F32)
    expert_w = (w_e_gate.astype(BF16), w_e_up.astype(BF16), w_e_down.astype(BF16))

    x_lat = x.reshape(B * S, D_MODEL)
    x_ctx = ctx.reshape(B * SC, D_MODEL)
    tpm_l = S
    tpm_c = SC
    for l in range(DEPTH):
        ctx_out = l < DEPTH - 1
        lw = _layer_weights(l, w_in, q_norm_g, w_uq, kv_norm_g, w_ukv, lru_wa, lru_ba, lru_wx, lru_bx,
                            lru_lambda, w_branch, w_out, w_router, rot)
        cv = (cv_w[l], cv_b[l][None, :], cv_ln_g[l][None, :], cv_ln_b[l][None, :])
        lru_conv = (lru_conv_w[l], lru_conv_b[l][None, :])
        ln1 = (ln1_g[l][None, :], ln1_b[l][None, :])
        ln2 = (ln2_g[l][None, :], ln2_b[l][None, :])
        lw_e = (*expert_w, l)

        mod = jax.nn.silu(c) @ ada_w[l] + ada_b[l]
        mods_l = [m[:, None, :] for m in jnp.split(mod, 6, axis=-1)]
        mod_c = jax.nn.silu(c_ctx) @ ada_w[l] + ada_b[l]
        mods_c = [m[None, None, :] for m in jnp.split(mod_c, 6)]

        proj_c = in_proj(x_ctx, mods_c[0], mods_c[1], lw['w1'], wcs, SC)
        q_c, k_c, v_c, conf_c, fnet_c, hrec_c, states = _sequence_mixers(
            B, SC, proj_c, lw, cv, lru_conv, tabs_c, dft_c, (zero_state, zero_state), need_branches=ctx_out)
        proj_l = in_proj(x_lat, mods_l[0], mods_l[1], lw['w1'], wcs, S)
        q_l, k_l, v_l, conf_l, fnet_l, hrec_l, _ = _sequence_mixers(
            B, S, proj_l, lw, cv, lru_conv, tabs_l, dft_l, states)

        attn_l = attention(q_l, [(k_l, v_l), (k_c, v_c)])
        merge_w = (lw['wgl'], lw['wb'], lw['wout'], *ln1, lw['wrh'], lw['wrl'])
        sel = lambda ms: (ms[0], ms[1], ms[2], ms[3], ms[4])
        y_l, h2_l, aff_l = merge(x_lat, sel(mods_l),
                                 [attn_l.reshape(B * S, -1), conf_l.reshape(B * S, -1),
                                  fnet_l.reshape(B * S, -1), hrec_l.reshape(B * S, -1)], *merge_w, tpm_l)
        x_lat = _moe(y_l, h2_l, aff_l, mods_l[5], lw_e, ln2, tri, B, S)
        if ctx_out:
            attn_c = attention(q_c, [(k_c, v_c)])
            y_c, h2_c, aff_c = merge(x_ctx, sel(mods_c),
                                     [attn_c.reshape(B * SC, -1), conf_c.reshape(B * SC, -1),
                                      fnet_c.reshape(B * SC, -1), hrec_c.reshape(B * SC, -1)], *merge_w, tpm_c)
            x_ctx = _moe(y_c, h2_c, aff_c, mods_c[5], lw_e, ln2, tri, B, SC)
    return x_lat.reshape(B, S, D_MODEL)
```

```python
import functools
import math

import jax
import jax.numpy as jnp
from jax import lax
from jax.experimental import pallas as pl
from jax.experimental.pallas import tpu as pltpu

D_MODEL = 1024
DEPTH = 4
GRID_W = 64
MLA_HEADS = 8
Q_LORA = 384
KV_LORA = 256
QK_NOPE = 64
QK_ROPE = 32
V_DIM = 64
ROPE_AXIS_FREQ = QK_ROPE // 4
ROPE_BASE = 10000.0
CONV_WIDTH = 512
CONV_K = 31
FNET_WIDTH = 512
FNET_GROUPS = 4
LRU_WIDTH = 512
LRU_HEADS = 8
LRU_CONV_K = 4
LRU_C = 8.0
N_BRANCH = 4
BRANCH_WIDTH = 512
N_EXPERTS = 16
EXPERT_FF = 1408
EC_CAPACITY = 2
DEEPNORM_ALPHA = (2 * DEPTH) ** 0.25
LN_EPS = 1e-6

V7X_LANES = 128
V7X_SUBLANES = 8
V7X_VMEM_LIMIT_BYTES = 56 * 1024 * 1024
HEAD_SLOT = V7X_LANES
TOKEN_TILE = 256
IN_PROJ_TILE = 512
MERGE_TILE = 256
SEQ_CHUNK = 512
ATTN_Q_ROWS = 128
ATTN_KEY_CHUNK = 256
EXPERT_TOKEN_BLOCK = 256
EXPERT_MIN_ROWS = 128
EXPERT_SMALL_FILL = 64
COMBINE_TOKEN_BLOCK = 128
EXPERT_ALIGN_LOG2 = 4
KVIN_WIDTH = KV_LORA + V7X_LANES
IN_PROJ_WIDTH = KVIN_WIDTH + LRU_WIDTH + Q_LORA + 2 * CONV_WIDTH + FNET_WIDTH + LRU_WIDTH

F32 = jnp.float32
BF16 = jnp.bfloat16


def _params(*sem):
    return pltpu.CompilerParams(dimension_semantics=sem, vmem_limit_bytes=V7X_VMEM_LIMIT_BYTES)


def _sigmoid(x):
    return 1.0 / (1.0 + jnp.exp(-x))


def _ln(x):
    mu = jnp.mean(x, axis=-1, keepdims=True)
    xc = x - mu
    var = jnp.mean(xc * xc, axis=-1, keepdims=True)
    return xc * lax.rsqrt(var + LN_EPS)


def _rms(x, g):
    return x * lax.rsqrt(jnp.mean(x * x, axis=-1, keepdims=True) + LN_EPS) * g


def _dot(a, b):
    return jnp.dot(a, b, preferred_element_type=F32)


def _mod_spec(mods, tiles_per_mod):
    if mods.shape[0] == 1:
        return pl.BlockSpec((1, 1, D_MODEL), lambda i: (0, 0, 0))
    return pl.BlockSpec((1, 1, D_MODEL), lambda i: (i // tiles_per_mod, 0, 0))


def _full_spec(arr):
    zeros = (0,) * arr.ndim
    return pl.BlockSpec(arr.shape, lambda *_: zeros)


def _in_proj_body(x_ref, sh_ref, sc_ref, w_ref, wcs_ref,
                  kvin_ref, xb_ref, cq_ref, u_ref, fa_ref, fb_ref, gb_ref):
    h = _ln(x_ref[...]) * (1.0 + sc_ref[0]) + sh_ref[0]
    p = _dot(h.astype(BF16), w_ref[...])
    o = 0
    kvin_ref[...] = p[:, o:o + KVIN_WIDTH]
    o += KVIN_WIDTH
    xb_ref[...] = p[:, o:o + LRU_WIDTH]
    o += LRU_WIDTH
    cq_ref[...] = p[:, o:o + Q_LORA]
    o += Q_LORA
    a = p[:, o:o + CONV_WIDTH]
    g = p[:, o + CONV_WIDTH:o + 2 * CONV_WIDTH]
    u_ref[...] = a * _sigmoid(g)
    o += 2 * CONV_WIDTH
    f = p[:, o:o + FNET_WIDTH].astype(BF16)
    o += FNET_WIDTH
    gb_ref[...] = p[:, o:o + LRU_WIDTH]
    ab = _dot(f, wcs_ref[...])
    fa_ref[...] = ab[:, :FNET_WIDTH].astype(BF16)
    fb_ref[...] = ab[:, FNET_WIDTH:].astype(BF16)


def in_proj(x, shift, scale, w1, wcs, rows_per_mod):
    M = x.shape[0]
    tm = min(IN_PROJ_TILE, rows_per_mod)
    tiles_per_mod = rows_per_mod // tm
    row = lambda i: (i, 0)
    widths = (KVIN_WIDTH, LRU_WIDTH, Q_LORA, CONV_WIDTH, FNET_WIDTH, FNET_WIDTH, LRU_WIDTH)
    dtypes = (F32, F32, F32, F32, BF16, BF16, F32)
    return pl.pallas_call(
        _in_proj_body,
        grid=(M // tm,),
        in_specs=[pl.BlockSpec((tm, D_MODEL), row), _mod_spec(shift, tiles_per_mod),
                  _mod_spec(scale, tiles_per_mod), _full_spec(w1), _full_spec(wcs)],
        out_specs=[pl.BlockSpec((tm, w), row) for w in widths],
        out_shape=[jax.ShapeDtypeStruct((M, w), d) for w, d in zip(widths, dtypes)],
        compiler_params=_params("arbitrary"),
        name="in_proj",
    )(x, shift, scale, w1, wcs)


def _qkv_body(kvin_ref, cq_ref, gq_ref, gkv_ref, wq1_ref, wq2_ref, wkn_ref, wkr_ref, wv_ref,
              qc_ref, qs_ref, kcs_ref, q_ref, k_ref, v_ref):
    nq = _rms(cq_ref[...], gq_ref[...]).astype(BF16)
    q1 = _dot(nq, wq1_ref[...])
    q2 = _dot(nq, wq2_ref[...])
    kvin = kvin_ref[...]
    n = _rms(kvin[:, :KV_LORA], gkv_ref[...]).astype(BF16)
    kr = (kvin[:, KV_LORA:] * kcs_ref[...]).astype(BF16)
    kk = _dot(n, wkn_ref[...]) + _dot(kr, wkr_ref[...])
    v = _dot(n, wv_ref[...])
    qc = qc_ref[...]
    qs = qs_ref[...]
    scale = (QK_NOPE + QK_ROPE) ** -0.5 * math.log2(math.e)
    for h in range(MLA_HEADS):
        sl = slice(h * HEAD_SLOT, (h + 1) * HEAD_SLOT)
        q_ref[0, h] = ((q1[:, sl] * qc + q2[:, sl] * qs) * scale).astype(BF16)
        k_ref[0, h] = kk[:, sl].astype(BF16)
    for p in range(MLA_HEADS // 2):
        v_ref[0, p] = v[:, p * V7X_LANES:(p + 1) * V7X_LANES].astype(BF16)


def qkv_proj(kvin, cq, gq, gkv, wq1, wq2, wkn, wkr, wv, qc, qs, kcs, B, S):
    tm = min(S, IN_PROJ_TILE)
    tps = S // tm
    row = lambda i: (i, 0)
    tab = lambda i: (i % tps, 0)
    seq = lambda i: (i // tps, 0, i % tps, 0)
    return pl.pallas_call(
        _qkv_body,
        grid=(B * tps,),
        in_specs=[pl.BlockSpec((tm, KVIN_WIDTH), row), pl.BlockSpec((tm, Q_LORA), row),
                  _full_spec(gq), _full_spec(gkv), _full_spec(wq1), _full_spec(wq2),
                  _full_spec(wkn), _full_spec(wkr), _full_spec(wv),
                  pl.BlockSpec((tm, HEAD_SLOT), tab), pl.BlockSpec((tm, HEAD_SLOT), tab),
                  pl.BlockSpec((tm, V7X_LANES), tab)],
        out_specs=[pl.BlockSpec((1, MLA_HEADS, tm, HEAD_SLOT), seq),
                   pl.BlockSpec((1, MLA_HEADS, tm, HEAD_SLOT), seq),
                   pl.BlockSpec((1, MLA_HEADS // 2, tm, V7X_LANES), seq)],
        out_shape=[jax.ShapeDtypeStruct((B, MLA_HEADS, S, HEAD_SLOT), BF16),
                   jax.ShapeDtypeStruct((B, MLA_HEADS, S, HEAD_SLOT), BF16),
                   jax.ShapeDtypeStruct((B, MLA_HEADS // 2, S, V7X_LANES), BF16)],
        compiler_params=_params("arbitrary"),
        name="qkv_proj",
    )(kvin, cq, gq, gkv, wq1, wq2, wkn, wkr, wv, qc, qs, kcs)


def _attn_body(*refs, n_sets):
    q_ref = refs[0]
    k_refs = refs[1:1 + 2 * n_sets:2]
    v_refs = refs[2:2 + 2 * n_sets:2]
    o_ref = refs[-1]
    tq = q_ref.shape[2]
    for hp in range(MLA_HEADS // 2):
        for r0 in range(0, tq, ATTN_Q_ROWS):
            rows = slice(r0, r0 + ATTN_Q_ROWS)
            outs = []
            for hh in range(2):
                h = 2 * hp + hh
                q = q_ref[0, h, rows, :]
                m = l = acc = None
                for k_ref, v_ref in zip(k_refs, v_refs):
                    for c0 in range(0, k_ref.shape[2], ATTN_KEY_CHUNK):
                        keys = slice(c0, c0 + ATTN_KEY_CHUNK)
                        s = lax.dot_general(q, k_ref[0, h, keys, :], (((1,), (1,)), ((), ())),
                                            preferred_element_type=F32)
                        m_c = jnp.max(s, axis=-1, keepdims=True)
                        m_new = m_c if m is None else jnp.maximum(m, m_c)
                        p = jnp.exp2(s - m_new)
                        l_c = functools.reduce(
                            jnp.add, [p[:, t:t + V7X_LANES] for t in range(0, ATTN_KEY_CHUNK, V7X_LANES)])
                        o_c = _dot(p.astype(BF16), v_ref[0, hp, keys, :])
                        if m is None:
                            l, acc = l_c, o_c
                        else:
                            alpha = jnp.exp2(m - m_new)
                            l, acc = alpha * l + l_c, alpha * acc + o_c
                        m = m_new
                outs.append(acc / jnp.sum(l, axis=-1, keepdims=True))
            lane = lax.broadcasted_iota(jnp.int32, outs[0].shape, 1)
            o_ref[0, rows, hp * V7X_LANES:(hp + 1) * V7X_LANES] = (
                jnp.where(lane < V_DIM, outs[0], outs[1]).astype(BF16))


def attention(q, kv_sets):
    B, H, S, _ = q.shape
    tq = TOKEN_TILE
    in_specs = [pl.BlockSpec((1, H, tq, HEAD_SLOT), lambda b, i: (b, 0, i, 0))]
    args = [q]
    for k, v in kv_sets:
        sk = k.shape[2]
        in_specs.append(pl.BlockSpec((1, H, sk, HEAD_SLOT), lambda b, i: (b, 0, 0, 0)))
        in_specs.append(pl.BlockSpec((1, H // 2, sk, V7X_LANES), lambda b, i: (b, 0, 0, 0)))
        args += [k, v]
    return pl.pallas_call(
        functools.partial(_attn_body, n_sets=len(kv_sets)),
        grid=(B, S // tq),
        in_specs=in_specs,
        out_specs=pl.BlockSpec((1, tq, H * V_DIM), lambda b, i: (b, i, 0)),
        out_shape=jax.ShapeDtypeStruct((B, S, H * V_DIM), BF16),
        compiler_params=_params("arbitrary", "arbitrary"),
        name="attention",
    )(*args)


def _fill_window(win_ref, x_ref, j, n_chunks, tc, halo):
    c = win_ref.shape[1]
    start = pl.multiple_of(j * tc, tc)
    win_ref[halo:halo + tc, :] = x_ref[0, pl.ds(start, tc), :]
    lo = pl.multiple_of(jnp.maximum(j * tc - halo, 0), halo)
    prev = x_ref[0, pl.ds(lo, halo), :]
    win_ref[0:halo, :] = jnp.where(j > 0, prev, jnp.zeros((halo, c), F32))
    hi = pl.multiple_of(jnp.minimum((j + 1) * tc, (n_chunks - 1) * tc + tc - halo), halo)
    nxt = x_ref[0, pl.ds(hi, halo), :]
    win_ref[halo + tc:2 * halo + tc, :] = jnp.where(j < n_chunks - 1, nxt, jnp.zeros((halo, c), F32))


def _conv_taps(win_ref, w_ref, b_ref, out_ref, tc, first_row, n_taps, row_block=64):
    c = win_ref.shape[1]
    for r0 in range(0, tc, row_block):
        for l0 in range(0, c, V7X_LANES):
            ls = slice(l0, l0 + V7X_LANES)
            acc = jnp.broadcast_to(b_ref[:, ls], (row_block, V7X_LANES))
            for k in range(n_taps):
                r = first_row + r0 + k
                acc = acc + win_ref[r:r + row_block, ls] * w_ref[k:k + 1, ls]
            out_ref[r0:r0 + row_block, ls] = acc


CONF_HALO = 16
CONF_ROW_BLOCK = 64


def _conf_body(u_ref, w_ref, b_ref, g_ref, beta_ref, o_ref, win_ref, sh_ref, acc_ref, *, n_chunks, tc):
    j = pl.program_id(1)
    _fill_window(win_ref, u_ref, j, n_chunks, tc, CONF_HALO)
    span = sh_ref.shape[1]
    for s in range(V7X_SUBLANES):
        sh_ref[s] = win_ref[s:s + span, :]
    first_row = CONF_HALO - (CONV_K - 1) // 2
    c = win_ref.shape[1]
    for r0 in range(0, tc, CONF_ROW_BLOCK):
        for l0 in range(0, c, V7X_LANES):
            ls = slice(l0, l0 + V7X_LANES)
            acc = jnp.broadcast_to(b_ref[:, ls], (CONF_ROW_BLOCK, V7X_LANES))
            for k in range(CONV_K):
                phase = (first_row + k) % V7X_SUBLANES
                base = r0 + (first_row + k) // V7X_SUBLANES * V7X_SUBLANES
                acc = acc + sh_ref[phase, base:base + CONF_ROW_BLOCK, ls] * w_ref[k:k + 1, ls]
            acc_ref[r0:r0 + CONF_ROW_BLOCK, ls] = acc
    y = _ln(acc_ref[...]) * g_ref[...] + beta_ref[...]
    o_ref[0] = (y * _sigmoid(y)).astype(BF16)


def conformer(u, w, b, g, beta):
    B, S, C = u.shape
    tc = min(S, SEQ_CHUNK)
    n_chunks = S // tc
    span = tc + 2 * CONF_HALO - V7X_SUBLANES
    return pl.pallas_call(
        functools.partial(_conf_body, n_chunks=n_chunks, tc=tc),
        grid=(B, n_chunks),
        in_specs=[pl.BlockSpec((1, S, C), lambda bb, j: (bb, 0, 0)),
                  _full_spec(w), _full_spec(b), _full_spec(g), _full_spec(beta)],
        out_specs=pl.BlockSpec((1, tc, C), lambda bb, j: (bb, j, 0)),
        out_shape=jax.ShapeDtypeStruct((B, S, C), BF16),
        scratch_shapes=[pltpu.VMEM((tc + 2 * CONF_HALO, C), F32), pltpu.VMEM((V7X_SUBLANES, span, C), F32),
                        pltpu.VMEM((tc, C), F32)],
        compiler_params=_params("arbitrary", "arbitrary"),
        name="conformer",
    )(u, w, b, g, beta)


def _fnet_body(c_ref, s_ref, a_ref, b_ref, o_ref, *, norm):
    y = _dot(c_ref[...], a_ref[0]) + _dot(s_ref[...], b_ref[0])
    o_ref[0] = (y * norm).astype(BF16)


def fnet_seq_dft(cmat, smat, fa, fb):
    B, S, W = fa.shape
    ts = min(S, 512)
    norm = 1.0 / math.sqrt(S * (FNET_WIDTH // FNET_GROUPS))
    return pl.pallas_call(
        functools.partial(_fnet_body, norm=norm),
        grid=(S // ts, B),
        in_specs=[pl.BlockSpec((ts, S), lambda i, b: (i, 0)), pl.BlockSpec((ts, S), lambda i, b: (i, 0)),
                  pl.BlockSpec((1, S, W), lambda i, b: (b, 0, 0)), pl.BlockSpec((1, S, W), lambda i, b: (b, 0, 0))],
        out_specs=pl.BlockSpec((1, ts, W), lambda i, b: (b, i, 0)),
        out_shape=jax.ShapeDtypeStruct((B, S, W), BF16),
        compiler_params=_params("arbitrary", "arbitrary"),
        name="fnet_seq_dft",
    )(cmat, smat, fa, fb)


LRU_HALO = 8
LRU_GROUP = V7X_SUBLANES
LRU_UNROLL = 8


def _lru_body(*refs, n_chunks, tc, reverse):
    if reverse:
        (x_ref, cw_ref, cb_ref, wg_ref, bg_ref, ca_ref, h0_ref, hf_ref, gb_ref,
         o_ref, s_ref, win_ref, xc_ref, a_ref, u_ref, carry_ref) = refs
    else:
        (x_ref, cw_ref, cb_ref, wg_ref, bg_ref, ca_ref, h0_ref,
         o_ref, s_ref, win_ref, xc_ref, a_ref, u_ref, carry_ref) = refs
    step = pl.program_id(1)
    j = (n_chunks - 1 - step) if reverse else step

    @pl.when(step == 0)
    def _():
        carry_ref[...] = jnp.broadcast_to(h0_ref[0], carry_ref.shape)

    _fill_window(win_ref, x_ref, j, n_chunks, tc, LRU_HALO)
    _conv_taps(win_ref, cw_ref, cb_ref, xc_ref, tc, LRU_HALO - LRU_CONV_K // 2, LRU_CONV_K)
    xc = xc_ref[...]
    gates = _dot(xc.astype(BF16), wg_ref[...]) + bg_ref[...]
    r = _sigmoid(gates[:, :LRU_WIDTH])
    i = _sigmoid(gates[:, LRU_WIDTH:])
    log_a = ca_ref[...] * r
    a = jnp.exp(log_a)
    a_ref[...] = a
    u_ref[...] = jnp.sqrt(-jnp.tanh(log_a) * (a * a + 1.0)) * (i * xc)

    n_groups = tc // LRU_GROUP
    row = lax.broadcasted_iota(jnp.int32, (LRU_GROUP, LRU_WIDTH), 0)

    def group(gi, carry):
        g = (n_groups - 1 - gi) if reverse else gi
        rows = pl.ds(pl.multiple_of(g * LRU_GROUP, LRU_GROUP), LRU_GROUP)
        a = a_ref[rows, :]
        u = u_ref[rows, :]
        for sh in (1, 2, 4):
            if reverse:
                a_sh = pltpu.roll(a, LRU_GROUP - sh, 0)
                u_sh = pltpu.roll(u, LRU_GROUP - sh, 0)
                valid = row < LRU_GROUP - sh
            else:
                a_sh = pltpu.roll(a, sh, 0)
                u_sh = pltpu.roll(u, sh, 0)
                valid = row >= sh
            a_sh = jnp.where(valid, a_sh, 1.0)
            u_sh = jnp.where(valid, u_sh, 0.0)
            u = a * u_sh + u
            a = a * a_sh
        h = u + a * carry
        u_ref[rows, :] = h
        last = h[0:1] if reverse else h[LRU_GROUP - 1:LRU_GROUP]
        return jnp.broadcast_to(last, h.shape)

    carry = lax.fori_loop(0, n_groups, group, carry_ref[...], unroll=LRU_UNROLL)
    carry_ref[...] = carry
    h = u_ref[...]
    if reverse:
        gb = gb_ref[0]
        gelu = 0.5 * gb * (1.0 + jnp.tanh(math.sqrt(2.0 / math.pi) * (gb + 0.044715 * (gb * gb * gb))))
        o_ref[0] = ((hf_ref[0] + h) * gelu).astype(BF16)
    else:
        o_ref[0] = h

    @pl.when(step == n_chunks - 1)
    def _():
        s_ref[0] = carry[0:1]


def lru_scan(xb, cw, cb, wg, bg, ca, h0, hf=None, gb=None):
    B, S, W = xb.shape
    reverse = hf is not None
    tc = min(S, SEQ_CHUNK)
    n_chunks = S // tc
    chunk = (lambda bb, s: (bb, n_chunks - 1 - s, 0)) if reverse else (lambda bb, s: (bb, s, 0))
    per_b = lambda bb, s: (bb, 0, 0)
    in_specs = [pl.BlockSpec((1, S, W), per_b), _full_spec(cw), _full_spec(cb), _full_spec(wg),
                _full_spec(bg), _full_spec(ca), pl.BlockSpec((1, 1, W), per_b)]
    args = [xb, cw, cb, wg, bg, ca, h0]
    if reverse:
        in_specs += [pl.BlockSpec((1, tc, W), chunk), pl.BlockSpec((1, tc, W), chunk)]
        args += [hf, gb]
    return pl.pallas_call(
        functools.partial(_lru_body, n_chunks=n_chunks, tc=tc, reverse=reverse),
        grid=(B, n_chunks),
        in_specs=in_specs,
        out_specs=[pl.BlockSpec((1, tc, W), chunk), pl.BlockSpec((1, 1, W), per_b)],
        out_shape=[jax.ShapeDtypeStruct((B, S, W), BF16 if reverse else F32),
                   jax.ShapeDtypeStruct((B, 1, W), F32)],
        scratch_shapes=[pltpu.VMEM((tc + 2 * LRU_HALO, W), F32), pltpu.VMEM((tc, W), F32),
                        pltpu.VMEM((tc, W), F32), pltpu.VMEM((tc, W), F32),
                        pltpu.VMEM((LRU_GROUP, W), F32)],
        compiler_params=_params("arbitrary", "arbitrary"),
        name="lru_bwd" if reverse else "lru_fwd",
    )(*args)


def _split_bf16(x):
    hi = x.astype(BF16)
    return hi, (x - hi.astype(F32)).astype(BF16)


def _merge_body(x_ref, sh1_ref, sc1_ref, g1_ref, sh2_ref, sc2_ref, attn_ref, conf_ref, fnet_ref, lru_ref,
                wgl_ref, wb_ref, wout_ref, lng_ref, lnb_ref, wrh_ref, wrl_ref,
                y_ref, h2_ref, aff_ref):
    x = x_ref[...]
    h = (_ln(x) * (1.0 + sc1_ref[0]) + sh1_ref[0]).astype(BF16)
    acc = None
    for k, br_ref in enumerate((attn_ref, conf_ref, fnet_ref, lru_ref)):
        gl = _dot(h, wgl_ref[:, k * D_MODEL:(k + 1) * D_MODEL])
        term = _sigmoid(gl) * _dot(br_ref[...], wb_ref[k])
        acc = term if acc is None else acc + term
    out = _dot(acc.astype(BF16), wout_ref[...])
    y = _ln(DEEPNORM_ALPHA * x + g1_ref[0] * out) * lng_ref[...] + lnb_ref[...]
    y_ref[...] = y
    h2 = _ln(y) * (1.0 + sc2_ref[0]) + sh2_ref[0]
    h2_ref[...] = h2.astype(BF16)
    hh, hl = _split_bf16(h2)
    logits = _dot(hh, wrh_ref[...]) + _dot(hh, wrl_ref[...]) + _dot(hl, wrh_ref[...])
    lane = lax.broadcasted_iota(jnp.int32, logits.shape, 1)
    valid = lane < N_EXPERTS
    logits = jnp.where(valid, logits, -jnp.inf)
    e = jnp.exp(logits - jnp.max(logits, axis=-1, keepdims=True))
    aff_ref[...] = e / jnp.sum(e, axis=-1, keepdims=True)


def merge(x, mods, branches, wgl, wb, wout, lng, lnb, wrh, wrl, rows_per_mod):
    M = x.shape[0]
    tm = min(MERGE_TILE, rows_per_mod)
    tiles_per_mod = rows_per_mod // tm
    row = lambda i: (i, 0)
    sh1, sc1, g1, sh2, sc2 = mods
    in_specs = ([pl.BlockSpec((tm, D_MODEL), row)] + [_mod_spec(m, tiles_per_mod) for m in mods]
                + [pl.BlockSpec((tm, BRANCH_WIDTH), row)] * N_BRANCH
                + [_full_spec(a) for a in (wgl, wb, wout, lng, lnb, wrh, wrl)])
    return pl.pallas_call(
        _merge_body,
        grid=(M // tm,),
        in_specs=in_specs,
        out_specs=[pl.BlockSpec((tm, D_MODEL), row), pl.BlockSpec((tm, D_MODEL), row),
                   pl.BlockSpec((tm, V7X_LANES), row)],
        out_shape=[jax.ShapeDtypeStruct((M, D_MODEL), F32), jax.ShapeDtypeStruct((M, D_MODEL), BF16),
                   jax.ShapeDtypeStruct((M, V7X_LANES), F32)],
        compiler_params=_params("arbitrary"),
        name="merge",
    )(x, sh1, sc1, g1, sh2, sc2, *branches, wgl, wb, wout, lng, lnb, wrh, wrl)


def _select_body(aff_ref, tri_ref, pos_ref, rng_ref, *, cap, n_tok, blk):
    aff = aff_ref[0]

    def bisect(it, lo):
        cand = lo | jnp.left_shift(jnp.int32(1), 30 - it)
        cnt = jnp.sum(jnp.where(aff >= pltpu.bitcast(cand, F32), 1.0, 0.0), axis=1, keepdims=True)
        return jnp.where(cnt >= cap, cand, lo)

    thr = pltpu.bitcast(lax.fori_loop(0, 31, bisect, jnp.zeros((N_EXPERTS, 1), jnp.int32)), F32)
    need = cap - jnp.sum(jnp.where(aff > thr, 1.0, 0.0), axis=1, keepdims=True)
    tri = tri_ref[...]
    ties_before = jnp.zeros((N_EXPERTS, 1), F32)
    sel_before = jnp.zeros((N_EXPERTS, 1), F32)
    lane = lax.broadcasted_iota(jnp.int32, (N_EXPERTS, V7X_LANES), 1)
    first_slot = jnp.zeros((N_EXPERTS, V7X_LANES), F32)
    for c0 in range(0, n_tok, V7X_LANES):
        sl = slice(c0, c0 + V7X_LANES)
        if c0 % blk == 0:
            first_slot = jnp.where(lane == c0 // blk, sel_before, first_slot)
        aff_c = aff[:, sl]
        gt_c = jnp.where(aff_c > thr, 1.0, 0.0)
        eq_c = jnp.where(aff_c == thr, 1.0, 0.0)
        tie_rank = _dot(eq_c.astype(BF16), tri) + ties_before
        ties_before = ties_before + jnp.sum(eq_c, axis=1, keepdims=True)
        sel_c = gt_c + eq_c * jnp.where(tie_rank < need, 1.0, 0.0)
        slot = _dot(sel_c.astype(BF16), tri) + sel_before
        sel_before = sel_before + jnp.sum(sel_c, axis=1, keepdims=True)
        pos_ref[0, :, sl] = jnp.where(sel_c > 0.0, slot, -1.0)

    rng_ref[0] = first_slot


def select_tokens(aff_t, tri, cap, blk):
    B, E, N = aff_t.shape
    return pl.pallas_call(
        functools.partial(_select_body, cap=cap, n_tok=N, blk=blk),
        grid=(B,),
        in_specs=[pl.BlockSpec((1, E, N), lambda b: (b, 0, 0)), _full_spec(tri)],
        out_specs=[pl.BlockSpec((1, E, N), lambda b: (b, 0, 0)),
                   pl.BlockSpec((1, E, V7X_LANES), lambda b: (b, 0, 0))],
        out_shape=[jax.ShapeDtypeStruct((B, E, N), F32), jax.ShapeDtypeStruct((B, E, V7X_LANES), F32)],
        compiler_params=_params("arbitrary"),
        name="select_tokens",
    )(aff_t, tri)


def _experts_body(first_ref, pos_ref, h_ref, wg_ref, wu_ref, wd_ref, y_ref, xe_ref,
                  *, cap, blk, n_blk, n_tok, bps, win, win_small):
    e = pl.program_id(0)
    xe_ref[...] = jnp.zeros(xe_ref.shape, F32)
    firsts, fills = [], []
    for bb in range(bps):
        base = ((pl.program_id(1) * bps + bb) * N_EXPERTS + e) * n_blk
        f = [first_ref[base + t] for t in range(n_blk)]
        firsts.append(f)
        fills += [nxt - cur for cur, nxt in zip(f, f[1:] + [cap])]

    def gather(rows):
        for bb in range(bps):
            for t in range(n_blk):
                s0 = lax.shift_left(lax.shift_right_logical(firsts[bb][t], EXPERT_ALIGN_LOG2), EXPERT_ALIGN_LOG2)
                slot = lax.broadcasted_iota(jnp.int32, (rows, 1), 0) + s0
                pos = pos_ref[0, bb, t].astype(jnp.int32)
                onehot = jnp.where(slot == pos, 1.0, 0.0).astype(BF16)
                tok = slice(bb * n_tok + t * blk, bb * n_tok + (t + 1) * blk)
                start = pl.multiple_of(s0 + bb * cap, 1 << EXPERT_ALIGN_LOG2)
                xe_ref[pl.ds(start, rows), :] += _dot(onehot, h_ref[tok, :])

    if win_small < win:
        narrow = functools.reduce(jnp.maximum, fills) <= EXPERT_SMALL_FILL
        pl.when(narrow)(lambda: gather(win_small))
        pl.when(jnp.logical_not(narrow))(lambda: gather(win))
    else:
        gather(win)
    xe = xe_ref[0:bps * cap, :].astype(BF16)
    g = _dot(xe, wg_ref[0, 0])
    u = _dot(xe, wu_ref[0, 0])
    he = (g * _sigmoid(g) * u).astype(BF16)
    y = _dot(he, wd_ref[0, 0]).astype(BF16)
    for bb in range(bps):
        y_ref[bb, 0] = y[bb * cap:(bb + 1) * cap]


def experts(first_slot, pos_blocks, h2, wg, wu, wd, layer, B, N, cap, blk):
    E = N_EXPERTS
    n_blk = N // blk
    bps = max(1, min(B, EXPERT_MIN_ROWS // cap))
    win = min(blk, cap) + (1 << EXPERT_ALIGN_LOG2)
    win_small = EXPERT_SMALL_FILL + (1 << EXPERT_ALIGN_LOG2)
    grid_spec = pltpu.PrefetchScalarGridSpec(
        num_scalar_prefetch=1,
        grid=(E, B // bps),
        in_specs=[pl.BlockSpec((1, bps, n_blk, 1, blk), lambda e, b, r: (e, b, 0, 0, 0)),
                  pl.BlockSpec((bps * N, D_MODEL), lambda e, b, r: (b, 0)),
                  pl.BlockSpec((1, 1, D_MODEL, EXPERT_FF), lambda e, b, r: (layer, e, 0, 0)),
                  pl.BlockSpec((1, 1, D_MODEL, EXPERT_FF), lambda e, b, r: (layer, e, 0, 0)),
                  pl.BlockSpec((1, 1, EXPERT_FF, D_MODEL), lambda e, b, r: (layer, e, 0, 0))],
        out_specs=pl.BlockSpec((bps, 1, cap, D_MODEL), lambda e, b, r: (b, e, 0, 0)),
        scratch_shapes=[pltpu.VMEM((bps * cap + win, D_MODEL), F32)])
    return pl.pallas_call(
        functools.partial(_experts_body, cap=cap, blk=blk, n_blk=n_blk, n_tok=N, bps=bps, win=win,
                          win_small=win_small),
        grid_spec=grid_spec,
        out_shape=jax.ShapeDtypeStruct((B, E, cap, D_MODEL), BF16),
        compiler_params=_params("arbitrary", "arbitrary"),
        name="experts",
    )(first_slot, pos_blocks, h2, wg, wu, wd)


def _combine_body(first_ref, x_ref, g2_ref, posc_ref, aff_ref, ye_ref, lng_ref, lnb_ref, o_ref,
                  *, cap, sub, win, tiles_per_batch, n_sub_batch):
    i = pl.program_id(0)
    b = i // tiles_per_batch
    sub_tiles = x_ref.shape[0] // sub
    first_sub = (i % tiles_per_batch) * sub_tiles
    lane_slot = lax.broadcasted_iota(jnp.int32, (1, win), 1)
    for st in range(sub_tiles):
        rows = slice(st * sub, (st + 1) * sub)
        posc = posc_ref[0, rows, :]
        aff = aff_ref[rows, :]
        acc = None
        for e in range(N_EXPERTS):
            first = first_ref[(b * N_EXPERTS + e) * n_sub_batch + first_sub + st]
            s0 = lax.shift_left(lax.shift_right_logical(first, EXPERT_ALIGN_LOG2), EXPERT_ALIGN_LOG2)
            s0 = pl.multiple_of(jnp.minimum(s0, cap - win), 1 << EXPERT_ALIGN_LOG2)
            pcol = posc[:, e:e + 1]
            onehot = jnp.where(pcol == (lane_slot + s0).astype(F32), 1.0, 0.0).astype(BF16)
            w = jnp.where(pcol >= 0.0, aff[:, e:e + 1], 0.0)
            term = w * _dot(onehot, ye_ref[0, e, pl.ds(s0, win), :])
            acc = term if acc is None else acc + term
        y = _ln(DEEPNORM_ALPHA * x_ref[rows, :] + g2_ref[0] * acc)
        o_ref[rows, :] = y * lng_ref[...] + lnb_ref[...]


def combine(first_slot, x, gate2, posc, aff, ye, lng, lnb, B, N, cap, sub):
    tm = min(N, 512)
    tpb = N // tm
    win = min(sub + (1 << EXPERT_ALIGN_LOG2), cap)
    row = lambda i, f: (i, 0)
    gate_spec = (pl.BlockSpec((1, 1, D_MODEL), lambda i, f: (0, 0, 0)) if gate2.shape[0] == 1
                 else pl.BlockSpec((1, 1, D_MODEL), lambda i, f: (i // tpb, 0, 0)))
    grid_spec = pltpu.PrefetchScalarGridSpec(
        num_scalar_prefetch=1,
        grid=(B * tpb,),
        in_specs=[pl.BlockSpec((tm, D_MODEL), row), gate_spec,
                  pl.BlockSpec((1, tm, V7X_LANES), lambda i, f: (i // tpb, i % tpb, 0)),
                  pl.BlockSpec((tm, V7X_LANES), row),
                  pl.BlockSpec((1, N_EXPERTS, cap, D_MODEL), lambda i, f: (i // tpb, 0, 0, 0)),
                  pl.BlockSpec(lng.shape, lambda i, f: (0, 0)), pl.BlockSpec(lnb.shape, lambda i, f: (0, 0))],
        out_specs=pl.BlockSpec((tm, D_MODEL), row))
    return pl.pallas_call(
        functools.partial(_combine_body, cap=cap, sub=sub, win=win, tiles_per_batch=tpb, n_sub_batch=N // sub),
        grid_spec=grid_spec,
        out_shape=jax.ShapeDtypeStruct((B * N, D_MODEL), F32),
        compiler_params=_params("arbitrary"),
        name="combine",
    )(first_slot, x, gate2, posc, aff, ye, lng, lnb)


def _rope_rotation():
    r = [[0.0] * QK_ROPE for _ in range(QK_ROPE)]
    half = QK_ROPE // 4
    for base in (0, QK_ROPE // 2):
        for j in range(half):
            r[base + j + half][base + j] = -1.0
            r[base + j][base + j + half] = 1.0
    return jnp.array(r, F32)


def _rope_tables(rows):
    row = jnp.repeat(jnp.arange(rows), GRID_W).astype(F32)
    col = jnp.tile(jnp.arange(GRID_W), rows).astype(F32)
    inv = ROPE_BASE ** (-jnp.arange(ROPE_AXIS_FREQ, dtype=F32) / ROPE_AXIS_FREQ)
    ang_r = row[:, None] * inv
    ang_c = col[:, None] * inv
    cos = jnp.concatenate([jnp.cos(ang_r)] * 2 + [jnp.cos(ang_c)] * 2, axis=-1)
    sin = jnp.concatenate([jnp.sin(ang_r)] * 2 + [jnp.sin(ang_c)] * 2, axis=-1)
    return cos, sin


def _position_tables(cos, sin):
    S = cos.shape[0]
    ones = jnp.ones((S, QK_NOPE), F32)
    pad = jnp.zeros((S, HEAD_SLOT - QK_NOPE - QK_ROPE), F32)
    qc = jnp.concatenate([ones, cos, pad], axis=-1)
    qs = jnp.concatenate([jnp.zeros((S, QK_NOPE), F32), sin, pad], axis=-1)
    kcs = jnp.concatenate([cos, sin, jnp.zeros((S, V7X_LANES - 2 * QK_ROPE), F32)], axis=-1)
    return qc, qs, kcs


def _dft_matrices(S):
    Q = 1 << (S.bit_length() // 2)
    P = S // Q
    j = jnp.arange(S, dtype=jnp.int32)[:, None]

    def table(step, n):
        ang = ((j * (step * jnp.arange(n, dtype=jnp.int32))[None, :]) % S).astype(F32) * (2.0 * math.pi / S)
        return jnp.cos(ang), jnp.sin(ang)

    c1, s1 = table(Q, P)
    c2, s2 = table(1, Q)
    cos = c1[:, :, None] * c2[:, None, :] - s1[:, :, None] * s2[:, None, :]
    sin = s1[:, :, None] * c2[:, None, :] + c1[:, :, None] * s2[:, None, :]
    return cos.reshape(S, S).astype(BF16), (-sin).reshape(S, S).astype(BF16)


def _channel_dft_weights():
    n = FNET_WIDTH // FNET_GROUPS
    k = jnp.arange(n, dtype=jnp.int32)
    ang = ((k[:, None] * k[None, :]) % n).astype(F32) * (2.0 * math.pi / n)
    eye = jnp.eye(FNET_GROUPS, dtype=F32)
    return jnp.concatenate([jnp.kron(eye, jnp.cos(ang)), jnp.kron(eye, jnp.sin(ang))], axis=1).astype(BF16)


def _block_diag_heads(w):
    H, d, _ = w.shape
    out = jnp.zeros((H, d, H, d), w.dtype)
    out = out.at[jnp.arange(H), :, jnp.arange(H), :].set(w)
    return out.reshape(H * d, H * d)


def _layer_weights(l, w_in, q_norm_g, w_uq, kv_norm_g, w_ukv, lru_wa, lru_ba, lru_wx, lru_bx, lru_lambda,
                   w_branch, w_out, w_router, rot):
    wi = w_in[l]
    o = 0
    ckv = wi[:, o:o + KV_LORA]
    o += KV_LORA
    kr = wi[:, o:o + QK_ROPE]
    o += QK_ROPE
    xb = wi[:, o:o + LRU_WIDTH]
    o += LRU_WIDTH
    cq = wi[:, o:o + Q_LORA]
    o += Q_LORA
    glu = wi[:, o:o + 2 * CONV_WIDTH]
    o += 2 * CONV_WIDTH
    f = wi[:, o:o + FNET_WIDTH]
    o += FNET_WIDTH
    gb = wi[:, o:o + LRU_WIDTH]
    o += LRU_WIDTH
    gl = wi[:, o:]
    kpad = jnp.zeros((D_MODEL, V7X_LANES - 2 * QK_ROPE), F32)
    w1 = jnp.concatenate([ckv, kr, kr @ rot, kpad, xb, cq, glu, f, gb], axis=1).astype(BF16)

    uq = w_uq[l]
    zq = jnp.zeros((Q_LORA, MLA_HEADS, HEAD_SLOT - QK_NOPE - QK_ROPE), F32)
    wq1 = jnp.concatenate([uq, zq], axis=-1).reshape(Q_LORA, MLA_HEADS * HEAD_SLOT).astype(BF16)
    uq_rot = jnp.einsum('rhd,de->rhe', uq[..., QK_NOPE:], rot)
    wq2 = jnp.concatenate([jnp.zeros((Q_LORA, MLA_HEADS, QK_NOPE), F32), uq_rot, zq], axis=-1)
    wq2 = wq2.reshape(Q_LORA, MLA_HEADS * HEAD_SLOT).astype(BF16)
    ukv = w_ukv[l]
    zk = jnp.zeros((KV_LORA, MLA_HEADS, HEAD_SLOT - QK_NOPE), F32)
    wkn = jnp.concatenate([ukv[..., :QK_NOPE], zk], axis=-1).reshape(KV_LORA, MLA_HEADS * HEAD_SLOT).astype(BF16)
    wv = ukv[..., QK_NOPE:].reshape(KV_LORA, MLA_HEADS * V_DIM).astype(BF16)
    eye = jnp.eye(QK_ROPE, dtype=F32)
    slot = jnp.concatenate([jnp.zeros((QK_ROPE, QK_NOPE), F32), eye,
                            jnp.zeros((QK_ROPE, HEAD_SLOT - QK_NOPE - QK_ROPE), F32)], axis=1)
    slot = jnp.tile(slot, (1, MLA_HEADS))
    wkr = jnp.concatenate([slot, slot, jnp.zeros((V7X_LANES - 2 * QK_ROPE, MLA_HEADS * HEAD_SLOT), F32)],
                          axis=0).astype(BF16)

    lru = []
    for d in range(2):
        wg = jnp.concatenate([_block_diag_heads(lru_wa[l, d]), _block_diag_heads(lru_wx[l, d])], axis=1).astype(BF16)
        bg = jnp.concatenate([lru_ba[l, d], lru_bx[l, d]])[None, :]
        ca = (-LRU_C * jax.nn.softplus(-lru_lambda[l, d]))[None, :]
        lru.append((wg, bg, ca))

    wr = jnp.pad(w_router[l], ((0, 0), (0, V7X_LANES - N_EXPERTS)))
    wrh, wrl = _split_bf16(wr)
    return dict(w1=w1, wq1=wq1, wq2=wq2, wkn=wkn, wkr=wkr, wv=wv, gq=q_norm_g[l][None, :],
                gkv=kv_norm_g[l][None, :], lru=lru, wgl=gl.astype(BF16), wb=w_branch[l].astype(BF16),
                wout=w_out[l].astype(BF16), wrh=wrh, wrl=wrl)


def _sequence_mixers(B, S, proj, lw, cv, lru_conv, tabs, dft, h0, need_branches=True):
    kvin, xb, cq, u, fa, fb, gb = proj
    q, k, v = qkv_proj(kvin, cq, lw['gq'], lw['gkv'], lw['wq1'], lw['wq2'], lw['wkn'], lw['wkr'], lw['wv'],
                       *tabs, B, S)
    conf = fnet = None
    if need_branches:
        conf = conformer(u.reshape(B, S, CONV_WIDTH), *cv)
        fnet = fnet_seq_dft(*dft, fa.reshape(B, S, FNET_WIDTH), fb.reshape(B, S, FNET_WIDTH))
    xb3 = xb.reshape(B, S, LRU_WIDTH)
    hf, s_fwd = lru_scan(xb3, *lru_conv, *lw['lru'][0], h0[0])
    hrec, s_bwd = lru_scan(xb3, *lru_conv, *lw['lru'][1], h0[1], hf=hf, gb=gb.reshape(B, S, LRU_WIDTH))
    return q, k, v, conf, fnet, hrec, (s_fwd, s_bwd)


def _moe(y, h2, aff, gate2, lw_e, ln2, tri, B, N):
    cap = EC_CAPACITY * N // N_EXPERTS
    aff_t = jnp.swapaxes(aff.reshape(B, N, V7X_LANES)[:, :, :N_EXPERTS], 1, 2)
    blk, sub = EXPERT_TOKEN_BLOCK, COMBINE_TOKEN_BLOCK
    pos, first_slot = select_tokens(aff_t, tri, cap, sub)
    first_slot = first_slot[:, :, :N // sub].astype(jnp.int32)
    first_blk = first_slot[:, :, ::blk // sub].reshape(-1)
    pos_blocks = jnp.swapaxes(pos, 0, 1).reshape(N_EXPERTS, B, N // blk, 1, blk)
    ye = experts(first_blk, pos_blocks, h2, *lw_e, B, N, cap, blk)
    posc = jnp.pad(jnp.swapaxes(pos, 1, 2), ((0, 0), (0, 0), (0, V7X_LANES - N_EXPERTS)), constant_values=-1.0)
    return combine(first_slot.reshape(-1), y, gate2, posc, aff, ye, *ln2, B, N, cap, sub)


def kernel(x, c, ctx, c_ctx, ada_w, ada_b, w_in, q_norm_g, w_uq, kv_norm_g, w_ukv, cv_w, cv_b, cv_ln_g, cv_ln_b, lru_conv_w, lru_conv_b, lru_wa, lru_ba, lru_wx, lru_bx, lru_lambda, w_branch, w_out, ln1_g, ln1_b, w_router, w_e_gate, w_e_up, w_e_down, ln2_g, ln2_b):
    B, S, _ = x.shape
    SC = ctx.shape[1]
    rot = _rope_rotation()
    cos, sin = _rope_tables(S // GRID_W)
    tabs_l = _position_tables(cos, sin)
    tabs_c = _position_tables(jnp.ones((SC, QK_ROPE), F32), jnp.zeros((SC, QK_ROPE), F32))
    dft_l = _dft_matrices(S)
    dft_c = _dft_matrices(SC)
    wcs = _channel_dft_weights()
    tri = jnp.triu(jnp.ones((V7X_LANES, V7X_LANES), F32), k=1).astype(BF16)
    zero_state = jnp.zeros((B, 1, LRU_WIDTH), F32)
    expert_w = (w_e_gate.astype(BF16), w_e_up.astype(BF16), w_e_down.astype(BF16))

    x_lat = x.reshape(B * S, D_MODEL)
    x_ctx = ctx.reshape(B * SC, D_MODEL)
    tpm_l = S
    tpm_c = SC
    for l in range(DEPTH):
        ctx_out = l < DEPTH - 1
        lw = _layer_weights(l, w_in, q_norm_g, w_uq, kv_norm_g, w_ukv, lru_wa, lru_ba, lru_wx, lru_bx,
                            lru_lambda, w_branch, w_out, w_router, rot)
        cv = (cv_w[l], cv_b[l][None, :], cv_ln_g[l][None, :], cv_ln_b[l][None, :])
        lru_conv = (lru_conv_w[l], lru_conv_b[l][None, :])
        ln1 = (ln1_g[l][None, :], ln1_b[l][None, :])
        ln2 = (ln2_g[l][None, :], ln2_b[l][None, :])
        lw_e = (*expert_w, l)

        mod = jax.nn.silu(c) @ ada_w[l] + ada_b[l]
        mods_l = [m[:, None, :] for m in jnp.split(mod, 6, axis=-1)]
        mod_c = jax.nn.silu(c_ctx) @ ada_w[l] + ada_b[l]
        mods_c = [m[None, None, :] for m in jnp.split(mod_c, 6)]

        proj_c = in_proj(x_ctx, mods_c[0], mods_c[1], lw['w1'], wcs, SC)
        q_c, k_c, v_c, conf_c, fnet_c, hrec_c, states = _sequence_mixers(
            B, SC, proj_c, lw, cv, lru_conv, tabs_c, dft_c, (zero_state, zero_state), need_branches=ctx_out)
        proj_l = in_proj(x_lat, mods_l[0], mods_l[1], lw['w1'], wcs, S)
        q_l, k_l, v_l, conf_l, fnet_l, hrec_l, _ = _sequence_mixers(
            B, S, proj_l, lw, cv, lru_conv, tabs_l, dft_l, states)

        attn_l = attention(q_l, [(k_l, v_l), (k_c, v_c)])
        merge_w = (lw['wgl'], lw['wb'], lw['wout'], *ln1, lw['wrh'], lw['wrl'])
        sel = lambda ms: (ms[0], ms[1], ms[2], ms[3], ms[4])
        y_l, h2_l, aff_l = merge(x_lat, sel(mods_l),
                                 [attn_l.reshape(B * S, -1), conf_l.reshape(B * S, -1),
                                  fnet_l.reshape(B * S, -1), hrec_l.reshape(B * S, -1)], *merge_w, tpm_l)
        x_lat = _moe(y_l, h2_l, aff_l, mods_l[5], lw_e, ln2, tri, B, S)
        if ctx_out:
            attn_c = attention(q_c, [(k_c, v_c)])
            y_c, h2_c, aff_c = merge(x_ctx, sel(mods_c),
                                     [attn_c.reshape(B * SC, -1), conf_c.reshape(B * SC, -1),
                                      fnet_c.reshape(B * SC, -1), hrec_c.reshape(B * SC, -1)], *merge_w, tpm_c)
            x_ctx = _moe(y_c, h2_c, aff_c, mods_c[5], lw_e, ln2, tri, B, SC)
    return x_lat.reshape(B, S, D_MODEL)
```
